```python
import jax, jax.numpy as jnp
from jax import lax
import numpy as np

D_MODEL = 1024
BATCH = 16
SEQ = 2048
DEPTH = 2

D_MIX = D_MODEL
HEAD_DIM = 64
NSA_WIDTH = D_MIX // 2
CONV_WIDTH = D_MIX - NSA_WIDTH
N_HEADS = NSA_WIDTH // HEAD_DIM
N_KV = 2
GROUP = N_HEADS // N_KV
N_CONV_GROUPS = CONV_WIDTH // HEAD_DIM
KV_W = N_KV * HEAD_DIM
N_BRANCH = 3
ROT_DIM = HEAD_DIM // 4
ROPE_THETA = 500000.0
CMP_LEN = 32
CMP_STRIDE = 16
CMP_HIDDEN = 128
SEL_BLOCK = 64
N_SEL = 8
WINDOW = 512
Q_BLOCK = 64
CONV_K = 3
D_FF = 2816
EPS = 1e-6
N_IN = NSA_WIDTH + 6 * KV_W + N_HEADS * N_BRANCH + 3 * CONV_WIDTH

kernel_name = "hymba_nsa_shortconv_convffn_adaln"


def rmsnorm(x, g):
    xf = x.astype(jnp.float32)
    r = lax.rsqrt(jnp.mean(xf * xf, axis=-1, keepdims=True) + EPS)
    return (xf * r).astype(x.dtype) * g


def causal_dwconv(x, w):
    k_taps = w.shape[0]
    s = x.shape[1]
    xp = jnp.pad(x, ((0, 0), (k_taps - 1, 0), (0, 0)))
    return sum(xp[:, k:k + s] * w[k] for k in range(k_taps))


def partial_rope(x, pos):
    half = ROT_DIM // 2
    freqs = ROPE_THETA ** (-jnp.arange(half, dtype=jnp.float32) / half)
    ang = pos.astype(jnp.float32)[..., None] * freqs
    cos = jnp.cos(ang)[:, :, None, :]
    sin = jnp.sin(ang)[:, :, None, :]
    xf = x.astype(jnp.float32)
    x1, x2, rest = xf[..., :half], xf[..., half:ROT_DIM], xf[..., ROT_DIM:]
    out = jnp.concatenate([x1 * cos - x2 * sin, x2 * cos + x1 * sin, rest], axis=-1)
    return out.astype(x.dtype)


def masked_softmax(s, mask):
    s = jnp.where(mask, s.astype(jnp.float32), -jnp.inf)
    m = jnp.max(s, axis=-1, keepdims=True)
    m = jnp.where(jnp.isfinite(m), m, 0.0)
    p = jnp.exp(s - m)
    return p / jnp.maximum(jnp.sum(p, axis=-1, keepdims=True), 1e-30)


def compress_blocks(tok, pe, w1, w2):
    b, hk, s, dh = tok.shape
    n_cmp = (s - CMP_LEN) // CMP_STRIDE + 1
    idx = jnp.arange(n_cmp)[:, None] * CMP_STRIDE + jnp.arange(CMP_LEN)[None, :]
    blk = tok[:, :, idx] + pe
    blk = blk.reshape(b, hk, n_cmp, CMP_LEN * dh)
    return jax.nn.silu(blk @ w1) @ w2


def nsa_attention(q, k_c, v_c, k_s, v_s, k_w, v_w, gates,
                  pe_k, w1_k, w2_k, pe_v, w1_v, w2_v):
    b, s = q.shape[0], q.shape[1]
    scale = HEAD_DIM ** -0.5
    qh = q.reshape(b, s, N_KV, GROUP, HEAD_DIM).transpose(0, 2, 3, 1, 4)
    gh = jax.nn.sigmoid(gates.astype(jnp.float32)).reshape(b, s, N_KV, GROUP, N_BRANCH)
    gh = gh.transpose(0, 2, 3, 1, 4)

    kc = compress_blocks(k_c, pe_k, w1_k, w2_k)
    vc = compress_blocks(v_c, pe_v, w1_v, w2_v)
    n_cmp = kc.shape[2]
    cmp_start = jnp.arange(n_cmp) * CMP_STRIDE
    cmp_end = cmp_start + CMP_LEN - 1
    n_slc = s // SEL_BLOCK
    n_sel = min(N_SEL, n_slc)
    slc_start = jnp.arange(n_slc) * SEL_BLOCK
    overlap = ((cmp_start[:, None] < slc_start[None, :] + SEL_BLOCK)
               & (cmp_start[:, None] + CMP_LEN > slc_start[None, :])).astype(jnp.float32)

    ks_blk = k_s.reshape(b, N_KV, n_slc, SEL_BLOCK, HEAD_DIM)
    vs_blk = v_s.reshape(b, N_KV, n_slc, SEL_BLOCK, HEAD_DIM)
    kw_pad = jnp.pad(k_w, ((0, 0), (0, 0), (WINDOW, 0), (0, 0)))
    vw_pad = jnp.pad(v_w, ((0, 0), (0, 0), (WINDOW, 0), (0, 0)))
    gather_blocks = jax.vmap(jax.vmap(lambda kb, ix: kb[ix]))
    j_blk = jnp.arange(n_slc)

    def query_block(qi):
        t0 = qi * Q_BLOCK
        pos = t0 + jnp.arange(Q_BLOCK)
        qb = lax.dynamic_slice_in_dim(qh, t0, Q_BLOCK, axis=3) * scale
        gb = lax.dynamic_slice_in_dim(gh, t0, Q_BLOCK, axis=3)

        s_c = jnp.einsum('bhgqd,bhkd->bhgqk', qb, kc)
        p_c = masked_softmax(s_c, cmp_end[None, :] <= pos[:, None])
        o_c = jnp.einsum('bhgqk,bhkd->bhgqd', p_c.astype(vc.dtype), vc)

        imp = jnp.einsum('bhgqk,kj->bhqj', p_c, overlap)
        blk_q = pos // SEL_BLOCK
        causal = j_blk[None, :] <= blk_q[:, None]
        forced = causal & ((j_blk[None, :] == 0) | (j_blk[None, :] == blk_q[:, None])
                           | (j_blk[None, :] == blk_q[:, None] - 1))
        imp = jnp.where(forced, jnp.inf, jnp.where(causal, imp, -jnp.inf))
        top_val, top_idx = lax.top_k(imp, n_sel)
        kg = gather_blocks(ks_blk, top_idx)
        vg = gather_blocks(vs_blk, top_idx)
        kg = kg.reshape(b, N_KV, Q_BLOCK, n_sel * SEL_BLOCK, HEAD_DIM)
        vg = vg.reshape(b, N_KV, Q_BLOCK, n_sel * SEL_BLOCK, HEAD_DIM)
        tok_pos = top_idx[..., None] * SEL_BLOCK + jnp.arange(SEL_BLOCK)
        m_s = (tok_pos <= pos[None, None, :, None, None]) & (top_val > -jnp.inf)[..., None]
        m_s = m_s.reshape(b, N_KV, 1, Q_BLOCK, n_sel * SEL_BLOCK)
        s_s = jnp.einsum('bhgqd,bhqmd->bhgqm', qb, kg)
        p_s = masked_softmax(s_s, m_s)
        o_s = jnp.einsum('bhgqm,bhqmd->bhgqd', p_s.astype(vg.dtype), vg)

        kwb = lax.dynamic_slice_in_dim(kw_pad, t0, Q_BLOCK + WINDOW, axis=2)
        vwb = lax.dynamic_slice_in_dim(vw_pad, t0, Q_BLOCK + WINDOW, axis=2)
        key_pos = t0 - WINDOW + jnp.arange(Q_BLOCK + WINDOW)
        diff = pos[:, None] - key_pos[None, :]
        m_w = (key_pos[None, :] >= 0) & (diff >= 0) & (diff < WINDOW)
        s_w = jnp.einsum('bhgqd,bhkd->bhgqk', qb, kwb)
        p_w = masked_softmax(s_w, m_w)
        o_w = jnp.einsum('bhgqk,bhkd->bhgqd', p_w.astype(vwb.dtype), vwb)

        o = gb[..., 0:1] * o_c + gb[..., 1:2] * o_s + gb[..., 2:3] * o_w
        return o.transpose(0, 3, 1, 2, 4).reshape(b, Q_BLOCK, NSA_WIDTH).astype(q.dtype)

    out = lax.map(query_block, jnp.arange(s // Q_BLOCK))
    return out.transpose(1, 0, 2, 3).reshape(b, s, NSA_WIDTH)


def setup_inputs(seed: int = 0) -> dict:
    key = jax.random.key(seed)
    ks = jax.random.split(key, 24)
    f32 = jnp.float32

    def nrm(k, shape, scale):
        return jax.random.normal(k, shape, f32) * scale

    x = nrm(ks[0], (BATCH, SEQ, D_MODEL), 1.0)
    c = nrm(ks[1], (BATCH, D_MODEL), 1.0)
    offs = jax.random.randint(ks[2], (BATCH, 1), 0, 4096, dtype=jnp.int32)
    positions = (offs + jnp.arange(SEQ, dtype=jnp.int32)[None, :]).astype(jnp.int32)
    return {
        "x": x,
        "c": c,
        "positions": positions,
        "w_mod": nrm(ks[3], (DEPTH, D_MODEL, 6 * D_MODEL), 0.5 * D_MODEL ** -0.5),
        "b_mod": nrm(ks[4], (DEPTH, 6 * D_MODEL), 0.02),
        "norm1_g": 1.0 + nrm(ks[5], (DEPTH, D_MODEL), 0.02),
        "w_in": nrm(ks[6], (DEPTH, D_MODEL, N_IN), D_MODEL ** -0.5),
        "cmp_pe_k": nrm(ks[7], (DEPTH, CMP_LEN, HEAD_DIM), 0.1),
        "cmp_w1_k": nrm(ks[8], (DEPTH, CMP_LEN * HEAD_DIM, CMP_HIDDEN), (CMP_LEN * HEAD_DIM) ** -0.5),
        "cmp_w2_k": nrm(ks[9], (DEPTH, CMP_HIDDEN, HEAD_DIM), CMP_HIDDEN ** -0.5),
        "cmp_pe_v": nrm(ks[10], (DEPTH, CMP_LEN, HEAD_DIM), 0.1),
        "cmp_w1_v": nrm(ks[11], (DEPTH, CMP_LEN * HEAD_DIM, CMP_HIDDEN), (CMP_LEN * HEAD_DIM) ** -0.5),
        "cmp_w2_v": nrm(ks[12], (DEPTH, CMP_HIDDEN, HEAD_DIM), CMP_HIDDEN ** -0.5),
        "conv_w": nrm(ks[13], (DEPTH, CONV_K, CONV_WIDTH), CONV_K ** -0.5),
        "grp_g_attn": 1.0 + nrm(ks[14], (DEPTH, NSA_WIDTH), 0.02),
        "grp_g_conv": 1.0 + nrm(ks[15], (DEPTH, CONV_WIDTH), 0.02),
        "w_out": nrm(ks[16], (DEPTH, D_MIX, D_MODEL), D_MIX ** -0.5),
        "norm2_g": 1.0 + nrm(ks[17], (DEPTH, D_MODEL), 0.02),
        "ffn_up": nrm(ks[18], (DEPTH, D_MODEL, 2 * D_FF), D_MODEL ** -0.5),
        "ffn_conv": nrm(ks[19], (DEPTH, CONV_K, 2 * D_FF), CONV_K ** -0.5),
        "ffn_down": nrm(ks[20], (DEPTH, D_FF, D_MODEL), D_FF ** -0.5),
        "final_g": 1.0 + nrm(ks[21], (D_MODEL,), 0.02),
    }


def reference(x, c, positions, w_mod, b_mod, norm1_g, w_in,
              cmp_pe_k, cmp_w1_k, cmp_w2_k, cmp_pe_v, cmp_w1_v, cmp_w2_v,
              conv_w, grp_g_attn, grp_g_conv, w_out, norm2_g,
              ffn_up, ffn_conv, ffn_down, final_g):
    b, s, _ = x.shape
    sizes = [NSA_WIDTH] + [KV_W] * 6 + [N_HEADS * N_BRANCH] + [CONV_WIDTH] * 3
    split_at = np.cumsum(sizes)[:-1].tolist()
    c_act = jax.nn.silu(c)

    def heads(t, n):
        return t.reshape(b, s, n, HEAD_DIM)

    def kv_layout(t):
        return t.transpose(0, 2, 1, 3)

    for l in range(DEPTH):
        mod = c_act @ w_mod[l] + b_mod[l]
        sh1, sc1, g1, sh2, sc2, g2 = [m[:, None, :] for m in jnp.split(mod, 6, axis=-1)]

        h = rmsnorm(x, norm1_g[l]) * (1.0 + sc1) + sh1
        proj = h @ w_in[l]
        q, kc, vc, ksel, vsel, kwin, vwin, gts, cb, cc, ch = jnp.split(proj, split_at, axis=-1)
        q = partial_rope(heads(q, N_HEADS), positions)
        kc = kv_layout(partial_rope(heads(kc, N_KV), positions))
        ksel = kv_layout(partial_rope(heads(ksel, N_KV), positions))
        kwin = kv_layout(partial_rope(heads(kwin, N_KV), positions))
        vc = kv_layout(heads(vc, N_KV))
        vsel = kv_layout(heads(vsel, N_KV))
        vwin = kv_layout(heads(vwin, N_KV))
        attn = nsa_attention(q, kc, vc, ksel, vsel, kwin, vwin, gts,
                             cmp_pe_k[l], cmp_w1_k[l], cmp_w2_k[l],
                             cmp_pe_v[l], cmp_w1_v[l], cmp_w2_v[l])
        conv = cb * causal_dwconv(cc * ch, conv_w[l])
        mix = jnp.concatenate([rmsnorm(attn, grp_g_attn[l]), rmsnorm(conv, grp_g_conv[l])], axis=-1)
        x = x + g1 * (mix @ w_out[l])

        h = rmsnorm(x, norm2_g[l]) * (1.0 + sc2) + sh2
        u = causal_dwconv(h @ ffn_up[l], ffn_conv[l])
        a, v = jnp.split(u, 2, axis=-1)
        x = x + g2 * ((jax.nn.silu(a) * v) @ ffn_down[l])

    return rmsnorm(x, final_g)
```

```python
import functools

import jax
import jax.numpy as jnp
from jax import lax
from jax.experimental import pallas as pl
from jax.experimental.pallas import tpu as pltpu

F32 = jnp.float32
BF16 = jnp.bfloat16

HEAD_DIM = 64
N_HEADS = 8
N_KV = 2
GROUP = N_HEADS // N_KV
N_BRANCH = 3
NSA_WIDTH = N_HEADS * HEAD_DIM
CONV_WIDTH = 512
KV_W = N_KV * HEAD_DIM
ROT_DIM = HEAD_DIM // 4
ROPE_THETA = 500000.0
CMP_LEN = 32
CMP_STRIDE = 16
CMP_HIDDEN = 128
SEL_BLOCK = 64
N_SEL = 8
WINDOW = 512
CONV_K = 3
EPS = 1e-6

LANES = 128
GATE_PAD = 16
NEG = -1e30
VMEM_LIMIT = 56 * 1024 * 1024

COL_Q = 0
COL_KV = NSA_WIDTH
COL_CONV = COL_KV + 6 * KV_W
COL_GATE = COL_CONV + 3 * CONV_WIDTH
N_IN_PAD = COL_GATE + N_KV * LANES


def _dot(a, b):
    return jnp.dot(a, b, preferred_element_type=F32)


def _silu(v):
    return v * jax.nn.sigmoid(v)


def _params(*sem):
    return pltpu.CompilerParams(dimension_semantics=sem, vmem_limit_bytes=VMEM_LIMIT)


def _mod_kernel(c_ref, w_ref, b_ref, o_ref):
    c = c_ref[...]
    o_ref[...] = _dot(_silu(c).astype(BF16), w_ref[...].astype(BF16)) + b_ref[...]


def _modulation(c, w_mod, b_mod):
    depth, d, n = w_mod.shape
    b = c.shape[0]
    tn = d
    return pl.pallas_call(
        _mod_kernel,
        grid=(depth, n // tn),
        in_specs=[
            pl.BlockSpec((b, d), lambda l, j: (0, 0)),
            pl.BlockSpec((None, d, tn), lambda l, j: (l, 0, j)),
            pl.BlockSpec((None, 1, tn), lambda l, j: (l, 0, j)),
        ],
        out_specs=pl.BlockSpec((None, b, tn), lambda l, j: (l, 0, j)),
        out_shape=jax.ShapeDtypeStruct((depth, b, n), F32),
        compiler_params=_params("arbitrary", "arbitrary"),
        name="modulation",
    )(c, w_mod, b_mod.reshape(depth, 1, n))


def _rope_kernel(pos_ref, freq_ref, cos_ref, sin_ref):
    ang = pos_ref[0].astype(F32) * freq_ref[...]
    lane = lax.broadcasted_iota(jnp.int32, (1, LANES), 1) % HEAD_DIM
    s = jnp.sin(ang)
    cos_ref[0] = jnp.cos(ang)
    sin_ref[0] = jnp.where(lane < ROT_DIM // 2, -s, s)


def _rope_tables(positions):
    b, s = positions.shape
    half = ROT_DIM // 2
    freqs = ROPE_THETA ** (-jnp.arange(half, dtype=F32) / half)
    lane = jnp.arange(LANES) % HEAD_DIM
    freq_row = jnp.where(lane < ROT_DIM, freqs[lane % half], 0.0).reshape(1, LANES)
    ts = min(s, 1024)
    spec = pl.BlockSpec((1, ts, LANES), lambda i, j: (i, j, 0))
    return pl.pallas_call(
        _rope_kernel,
        grid=(b, s // ts),
        in_specs=[
            pl.BlockSpec((1, ts, 1), lambda i, j: (i, j, 0)),
            pl.BlockSpec((1, LANES), lambda i, j: (0, 0)),
        ],
        out_specs=[spec, spec],
        out_shape=[jax.ShapeDtypeStruct((b, s, LANES), F32)] * 2,
        compiler_params=_params("arbitrary", "arbitrary"),
        name="rope_tables",
    )(positions.reshape(b, s, 1), freq_row)


def _rms_mod(x, g, scale, shift):
    r = lax.rsqrt(jnp.mean(x * x, axis=-1, keepdims=True) + EPS)
    return (x * r) * g * (1.0 + scale) + shift


def _inproj_kernel(x_ref, mod_ref, g_ref, w_ref, cos_ref, sin_ref,
                   q_ref, kv_ref, conv_ref, gate_ref):
    h = _rms_mod(x_ref[0], g_ref[...], mod_ref[0, 1:2, :], mod_ref[0, 0:1, :]).astype(BF16)
    cos_t = cos_ref[0]
    sin_t = sin_ref[0]
    first = (lax.broadcasted_iota(jnp.int32, (1, LANES), 1) % HEAD_DIM) < ROT_DIM // 2

    def rope(y):
        partner = jnp.where(first, pltpu.roll(y, LANES - ROT_DIM // 2, 1),
                            pltpu.roll(y, ROT_DIM // 2, 1))
        return y * cos_t + partner * sin_t

    scale = HEAD_DIM ** -0.5
    yq = _dot(h, w_ref[:, COL_Q:COL_KV])
    for j in range(NSA_WIDTH // LANES):
        sl = slice(j * LANES, (j + 1) * LANES)
        q_ref[0, :, sl] = (rope(yq[:, sl]) * scale).astype(BF16)
    ykv = _dot(h, w_ref[:, COL_KV:COL_CONV])
    for j in range(6):
        sl = slice(j * LANES, (j + 1) * LANES)
        part = ykv[:, sl]
        kv_ref[0, :, sl] = (rope(part) if j % 2 == 0 else part).astype(BF16)
    conv_ref[0] = _dot(h, w_ref[:, COL_CONV:COL_GATE])
    gate_ref[0] = _dot(h, w_ref[:, COL_GATE:N_IN_PAD])


def _in_projection(x, mod, g, w, cos_t, sin_t, tm):
    b, s, d = x.shape
    row = lambda n: pl.BlockSpec((1, tm, n), lambda i, j: (i, j, 0))
    return pl.pallas_call(
        _inproj_kernel,
        grid=(b, s // tm),
        in_specs=[
            row(d),
            pl.BlockSpec((1, 6, d), lambda i, j: (i, 0, 0)),
            pl.BlockSpec((1, d), lambda i, j: (0, 0)),
            pl.BlockSpec((d, N_IN_PAD), lambda i, j: (0, 0)),
            row(LANES), row(LANES),
        ],
        out_specs=[row(NSA_WIDTH), row(6 * KV_W), row(3 * CONV_WIDTH), row(N_KV * LANES)],
        out_shape=[
            jax.ShapeDtypeStruct((b, s, NSA_WIDTH), BF16),
            jax.ShapeDtypeStruct((b, s, 6 * KV_W), BF16),
            jax.ShapeDtypeStruct((b, s, 3 * CONV_WIDTH), F32),
            jax.ShapeDtypeStruct((b, s, N_KV * LANES), F32),
        ],
        compiler_params=_params("arbitrary", "arbitrary"),
        name="in_projection",
    )(x, mod, g, w, cos_t, sin_t)


def _compress_kernel(r_ref, pe_ref, w1_ref, w2_ref, o_ref):
    r = r_ref[0, 0, 0].astype(F32)
    n_rows, half = r.shape
    lo = (r + pe_ref[0, 0:1, :]).astype(BF16)
    hi = (r + pe_ref[0, 1:2, :]).astype(BF16)
    y_lo = _dot(lo, w1_ref[0, :half, :])
    y_hi = _dot(hi, w1_ref[0, half:, :])
    pre = y_lo + pltpu.roll(y_hi, n_rows - 1, 0)
    out = _dot(_silu(pre).astype(BF16), w2_ref[0])
    rows = lax.broadcasted_iota(jnp.int32, out.shape, 0)
    o_ref[0, 0] = jnp.where(rows < n_rows - 1, out, 0.0)


def _compress(r, pe, w1, w2):
    b, _, _, n_rows, width = r.shape
    return pl.pallas_call(
        _compress_kernel,
        grid=(b, 2 * N_KV),
        in_specs=[
            pl.BlockSpec((1, 1, 1, n_rows, width), lambda i, j: (i, j // N_KV, j % N_KV, 0, 0)),
            pl.BlockSpec((1, 2, width), lambda i, j: (j // N_KV, 0, 0)),
            pl.BlockSpec((1, 2 * width, CMP_HIDDEN), lambda i, j: (j // N_KV, 0, 0)),
            pl.BlockSpec((1, CMP_HIDDEN, HEAD_DIM), lambda i, j: (j // N_KV, 0, 0)),
        ],
        out_specs=pl.BlockSpec((1, 1, n_rows, HEAD_DIM), lambda i, j: (i, j, 0, 0)),
        out_shape=jax.ShapeDtypeStruct((b, 2 * N_KV, n_rows, HEAD_DIM), F32),
        compiler_params=_params("arbitrary", "arbitrary"),
        name="compress",
    )(r, pe, w1, w2)


def _attn_kernel(qt_ref, kc_ref, vct_ref, ks_ref, vst_ref, kw_ref, vwt_ref, gt_ref, ov_ref,
                 o_ref, selb_ref, *, tq, tk, n_sel):
    t0 = pl.program_id(2) * tq
    rows = GROUP * tq
    qt = jnp.concatenate([qt_ref[0, g] for g in range(GROUP)], axis=1)
    t_lane = t0 + lax.broadcasted_iota(jnp.int32, (1, tq), 1)

    def over_group(m):
        return jnp.concatenate([m] * GROUP, axis=1)

    n_cmp = kc_ref.shape[2]
    s_c = _dot(kc_ref[0, 0], qt)
    cmp_end = lax.broadcasted_iota(jnp.int32, (n_cmp, 1), 0) * CMP_STRIDE + (CMP_LEN - 1)
    s_c = jnp.where(cmp_end <= over_group(t_lane), s_c, -jnp.inf)
    m_c = jnp.max(s_c, axis=0, keepdims=True)
    m_c = jnp.where(m_c == -jnp.inf, 0.0, m_c)
    p_c = jnp.exp(s_c - m_c)
    p_c = p_c / jnp.maximum(jnp.sum(p_c, axis=0, keepdims=True), 1e-30)
    o_c = _dot(vct_ref[0, 0], p_c.astype(BF16))
    p_sum = p_c[:, 0:tq]
    for g in range(1, GROUP):
        p_sum = p_sum + p_c[:, g * tq:(g + 1) * tq]
    p_hi = p_sum.astype(BF16)
    p_lo = (p_sum - p_hi.astype(F32)).astype(BF16)
    imp = _dot(ov_ref[...], p_hi) + _dot(ov_ref[...], p_lo)

    n_blk = imp.shape[0]
    j_blk = lax.broadcasted_iota(jnp.int32, (n_blk, tq), 0).astype(F32)
    blk_q = (t_lane // SEL_BLOCK).astype(F32)
    causal = j_blk <= blk_q
    forced = causal & ((j_blk == 0.0) | (j_blk == blk_q) | (j_blk == blk_q - 1.0))
    val = jnp.where(forced, jnp.inf, jnp.where(causal, imp, -jnp.inf))
    bias = jnp.full((n_blk, tq), NEG, F32)
    for _ in range(n_sel):
        top = jnp.max(val, axis=0, keepdims=True)
        cand = (val == top) & (val > -jnp.inf)
        first = jnp.min(jnp.where(cand, j_blk, float(n_blk)), axis=0, keepdims=True)
        pick = j_blk == first
        bias = jnp.where(pick, 0.0, bias)
        val = jnp.where(pick, -jnp.inf, val)
    selb_ref[...] = bias

    u_sub = lax.broadcasted_iota(jnp.int32, (tk, 1), 0)
    hi_tile = (t0 + tq - 1) // tk

    def flash_step(k_tile, vt_tile, mask_bias, carry):
        m, l, acc = carry
        s = _dot(k_tile, qt) + over_group(mask_bias)
        m_new = jnp.maximum(m, jnp.max(s, axis=0, keepdims=True))
        alpha = jnp.exp(m - m_new)
        p = jnp.exp(s - m_new)
        l = alpha * l + jnp.sum(p, axis=0, keepdims=True)
        acc = alpha * acc + _dot(vt_tile, p.astype(BF16))
        return m_new, l, acc

    init = (jnp.full((1, rows), NEG, F32), jnp.zeros((1, rows), F32),
            jnp.zeros((HEAD_DIM, rows), F32))

    def sel_body(i, carry):
        kt = hi_tile - i
        u = kt * tk + u_sub
        per_blk = tk // SEL_BLOCK
        blk_bias = jnp.concatenate(
            [jnp.broadcast_to(selb_ref[pl.ds(kt * per_blk + jj, 1), :], (SEL_BLOCK, tq))
             for jj in range(per_blk)], axis=0)
        mask_bias = jnp.where(u <= t_lane, blk_bias, NEG)
        return flash_step(ks_ref[0, 0, kt], vst_ref[0, 0, kt], mask_bias, carry)

    _, l_s, acc_s = lax.fori_loop(0, hi_tile + 1, sel_body, init)

    lo_tile = jnp.maximum(t0 - (WINDOW - 1), 0) // tk

    def win_body(i, carry):
        kt = hi_tile - i
        dist = t_lane - (kt * tk + u_sub)
        mask_bias = jnp.where((dist >= 0) & (dist < WINDOW), 0.0, NEG)
        return flash_step(kw_ref[0, 0, kt], vwt_ref[0, 0, kt], mask_bias, carry)

    _, l_w, acc_w = lax.fori_loop(0, hi_tile - lo_tile + 1, win_body, init)

    o_s = acc_s / jnp.maximum(l_s, 1e-30)
    o_w = acc_w / jnp.maximum(l_w, 1e-30)
    gate = jax.nn.sigmoid(gt_ref[0, 0])
    for g in range(GROUP):
        sl = slice(g * tq, (g + 1) * tq)
        o_ref[0, g] = (gate[g:g + 1] * o_c[:, sl]
                       + gate[GROUP + g:GROUP + g + 1] * o_s[:, sl]
                       + gate[2 * GROUP + g:2 * GROUP + g + 1] * o_w[:, sl])


def _attention(qt, kc, vct, ks, vst, kw, vwt, gt, ov, tq, tk):
    b, _, _, s = qt.shape
    n_cmp = kc.shape[2]
    n_kt = s // tk
    n_blk = ov.shape[0]
    n_sel = min(N_SEL, s // SEL_BLOCK)
    kern = functools.partial(_attn_kernel, tq=tq, tk=tk, n_sel=n_sel)
    k_spec = pl.BlockSpec((1, 1, n_kt, tk, HEAD_DIM), lambda i, h, j: (i, h, 0, 0, 0))
    vt_spec = pl.BlockSpec((1, 1, n_kt, HEAD_DIM, tk), lambda i, h, j: (i, h, 0, 0, 0))
    return pl.pallas_call(
        kern,
        grid=(b, N_KV, s // tq),
        in_specs=[
            pl.BlockSpec((1, GROUP, HEAD_DIM, tq), lambda i, h, j: (i, h, 0, j)),
            pl.BlockSpec((1, 1, n_cmp, HEAD_DIM), lambda i, h, j: (i, h, 0, 0)),
            pl.BlockSpec((1, 1, HEAD_DIM, n_cmp), lambda i, h, j: (i, h, 0, 0)),
            k_spec, vt_spec, k_spec, vt_spec,
            pl.BlockSpec((1, 1, GATE_PAD, tq), lambda i, h, j: (i, h, 0, j)),
            pl.BlockSpec((n_blk, n_cmp), lambda i, h, j: (0, 0)),
        ],
        out_specs=pl.BlockSpec((1, GROUP, HEAD_DIM, tq), lambda i, h, j: (i, h, 0, j)),
        out_shape=jax.ShapeDtypeStruct((b, N_HEADS, HEAD_DIM, s), F32),
        scratch_shapes=[pltpu.VMEM((n_blk, tq), F32)],
        compiler_params=_params("arbitrary", "arbitrary", "arbitrary"),
        name="nsa_attention",
    )(qt, kc, vct, ks, vst, kw, vwt, gt, ov)


def _shifted_taps(cur, prev, taps):
    t = cur.shape[0]
    ext = jnp.concatenate([prev, cur], axis=0)
    return taps[0:1] * ext[6:6 + t] + taps[1:2] * ext[7:7 + t] + taps[2:3] * cur


def _group_norm(v, g):
    r = lax.rsqrt(jnp.mean(v * v, axis=-1, keepdims=True) + EPS)
    return (v * r) * g


def _outproj_kernel(attn_ref, cb_ref, cc_ref, ch_ref, ccp_ref, chp_ref, x_ref, mod_ref,
                    ga_ref, gc_ref, cw_ref, w_ref, o_ref):
    z = cc_ref[0] * ch_ref[0]
    z_prev = jnp.where(pl.program_id(1) > 0, ccp_ref[0] * chp_ref[0], 0.0)
    conv = cb_ref[0] * _shifted_taps(z, z_prev, cw_ref[...])
    mix_a = _group_norm(attn_ref[0], ga_ref[...]).astype(BF16)
    mix_c = _group_norm(conv, gc_ref[...]).astype(BF16)
    y = _dot(mix_a, w_ref[:NSA_WIDTH, :]) + _dot(mix_c, w_ref[NSA_WIDTH:, :])
    o_ref[0] = x_ref[0] + mod_ref[0, 2:3, :] * y


def _out_projection(attn, conv3, x, mod, ga, gc, cw, w, tm):
    b, s, d = x.shape
    halo = tm // 8
    col = lambda c: pl.BlockSpec((1, tm, CONV_WIDTH), lambda i, j: (i, j, c))
    prev = lambda c: pl.BlockSpec((1, 8, CONV_WIDTH),
                                  lambda i, j: (i, jnp.maximum(j * halo - 1, 0), c))
    const = lambda shape: pl.BlockSpec(shape, lambda i, j: (0,) * len(shape))
    return pl.pallas_call(
        _outproj_kernel,
        grid=(b, s // tm),
        in_specs=[
            pl.BlockSpec((1, tm, NSA_WIDTH), lambda i, j: (i, j, 0)),
            col(0), col(1), col(2), prev(1), prev(2),
            pl.BlockSpec((1, tm, d), lambda i, j: (i, j, 0)),
            pl.BlockSpec((1, 6, d), lambda i, j: (i, 0, 0)),
            const((1, NSA_WIDTH)), const((1, CONV_WIDTH)), const((CONV_K, CONV_WIDTH)),
            const((NSA_WIDTH + CONV_WIDTH, d)),
        ],
        out_specs=pl.BlockSpec((1, tm, d), lambda i, j: (i, j, 0)),
        out_shape=jax.ShapeDtypeStruct((b, s, d), F32),
        compiler_params=_params("arbitrary", "arbitrary"),
        name="out_projection",
    )(attn, conv3, conv3, conv3, conv3, conv3, x, mod, ga, gc, cw, w)


def _ffn_kernel(x_ref, mod_ref, g_ref, wu_ref, cw_ref, wd_ref, fg_ref, o_ref,
                act_ref, carry_ref, *, d_ff, tf, final_norm):
    x = x_ref[0]
    h = _rms_mod(x, g_ref[...], mod_ref[0, 4:5, :], mod_ref[0, 3:4, :]).astype(BF16)

    @pl.when(pl.program_id(1) == 0)
    def _():
        carry_ref[...] = jnp.zeros_like(carry_ref)

    def conv_cols(cols):
        u = _dot(h, wu_ref[:, cols])
        prev = carry_ref[:, cols]
        carry_ref[:, cols] = u[u.shape[0] - 8:, :]
        return _shifted_taps(u, prev, cw_ref[:, cols])

    for c in range(d_ff // tf):
        gate = conv_cols(slice(c * tf, (c + 1) * tf))
        value = conv_cols(slice(d_ff + c * tf, d_ff + (c + 1) * tf))
        act_ref[:, c * tf:(c + 1) * tf] = (_silu(gate) * value).astype(BF16)
    y = _dot(act_ref[...], wd_ref[...])
    out = x + mod_ref[0, 5:6, :] * y
    if final_norm:
        out = _group_norm(out, fg_ref[...])
    o_ref[0] = out


def _ffn(x, mod, g, wu, cw, wd, fg, tm, final_norm):
    b, s, d = x.shape
    d_ff = wd.shape[0]
    tf = 256
    kern = functools.partial(_ffn_kernel, d_ff=d_ff, tf=tf, final_norm=final_norm)
    const = lambda shape: pl.BlockSpec(shape, lambda i, j: (0,) * len(shape),
                                       pipeline_mode=pl.Buffered(1))
    return pl.pallas_call(
        kern,
        grid=(b, s // tm),
        in_specs=[
            pl.BlockSpec((1, tm, d), lambda i, j: (i, j, 0)),
            pl.BlockSpec((1, 6, d), lambda i, j: (i, 0, 0)),
            const((1, d)), const((d, 2 * d_ff)), const((CONV_K, 2 * d_ff)), const((d_ff, d)),
            const((1, d)),
        ],
        out_specs=pl.BlockSpec((1, tm, d), lambda i, j: (i, j, 0)),
        out_shape=jax.ShapeDtypeStruct((b, s, d), F32),
        scratch_shapes=[pltpu.VMEM((tm, d_ff), BF16), pltpu.VMEM((8, 2 * d_ff), F32)],
        compiler_params=_params("arbitrary", "arbitrary"),
        name="conv_ffn",
    )(x, mod, g, wu, cw, wd, fg)


def _reorder_w_in(w):
    d = w.shape[0]
    o_gate = NSA_WIDTH + 6 * KV_W
    o_conv = o_gate + N_HEADS * N_BRANCH
    gates = w[:, o_gate:o_conv].reshape(d, N_KV, GROUP, N_BRANCH)
    gates = gates.transpose(0, 1, 3, 2).reshape(d, N_KV, N_BRANCH * GROUP)
    gates = jnp.pad(gates, ((0, 0), (0, 0), (0, LANES - N_BRANCH * GROUP)))
    return jnp.concatenate(
        [w[:, :o_gate], w[:, o_conv:], gates.reshape(d, N_KV * LANES)], axis=1).astype(BF16)


def _overlap_matrix(s):
    n_cmp = s // CMP_STRIDE
    n_blk = max(s // SEL_BLOCK, 8)
    cmp_start = jnp.arange(n_cmp) * CMP_STRIDE
    blk_start = jnp.arange(n_blk) * SEL_BLOCK
    ov = ((cmp_start[None, :] < blk_start[:, None] + SEL_BLOCK)
          & (cmp_start[None, :] + CMP_LEN > blk_start[:, None])
          & (jnp.arange(n_cmp)[None, :] < n_cmp - 1))
    return ov.astype(BF16)


def kernel(x, c, positions, w_mod, b_mod, norm1_g, w_in, cmp_pe_k, cmp_w1_k, cmp_w2_k,
           cmp_pe_v, cmp_w1_v, cmp_w2_v, conv_w, grp_g_attn, grp_g_conv, w_out, norm2_g,
           ffn_up, ffn_conv, ffn_down, final_g):
    b, s, d = x.shape
    depth = w_in.shape[0]
    tm = min(512, s)
    tq = 128
    tk = min(256, s)
    n_kt = s // tk
    n_rows = s // CMP_STRIDE

    mod_all = _modulation(c, w_mod, b_mod).reshape(depth, b, 6, d)
    cos_t, sin_t = _rope_tables(positions)
    ov = _overlap_matrix(s)

    for l in range(depth):
        mod = mod_all[l]
        q, kv, conv3, gates = _in_projection(
            x, mod, norm1_g[l].reshape(1, d), _reorder_w_in(w_in[l]), cos_t, sin_t, tm)

        qt = q.reshape(b, s, N_HEADS, HEAD_DIM).transpose(0, 2, 3, 1)
        kv = kv.reshape(b, s, 6, N_KV, HEAD_DIM)
        r = kv[:, :, 0:2].transpose(0, 2, 3, 1, 4).reshape(b, 2, N_KV, n_rows, CMP_STRIDE * HEAD_DIM)
        keys = lambda i: kv[:, :, i].transpose(0, 2, 1, 3).reshape(b, N_KV, n_kt, tk, HEAD_DIM)
        vals_t = lambda i: kv[:, :, i].reshape(b, n_kt, tk, N_KV, HEAD_DIM).transpose(0, 3, 1, 4, 2)
        gt = gates.reshape(b, s, N_KV, LANES)[..., :GATE_PAD].transpose(0, 2, 3, 1)

        half = CMP_LEN // 2
        pe = jnp.stack([cmp_pe_k[l], cmp_pe_v[l]]).reshape(2, 2, half * HEAD_DIM)
        w1 = jnp.stack([cmp_w1_k[l], cmp_w1_v[l]]).astype(BF16)
        w2 = jnp.stack([cmp_w2_k[l], cmp_w2_v[l]]).astype(BF16)
        cmp = _compress(r, pe, w1, w2)
        kc = cmp[:, :N_KV].astype(BF16)
        vct = cmp[:, N_KV:].transpose(0, 1, 3, 2).astype(BF16)

        attn_t = _attention(qt, kc, vct, keys(2), vals_t(3), keys(4), vals_t(5), gt, ov, tq, tk)
        attn = attn_t.transpose(0, 3, 1, 2).reshape(b, s, NSA_WIDTH)

        x = _out_projection(attn, conv3, x, mod, grp_g_attn[l].reshape(1, -1),
                            grp_g_conv[l].reshape(1, -1), conv_w[l], w_out[l].astype(BF16), tm)
        x = _ffn(x, mod, norm2_g[l].reshape(1, d), ffn_up[l].astype(BF16), ffn_conv[l],
                 ffn_down[l].astype(BF16), final_g.reshape(1, d), tm, l == depth - 1)
    return x
```

```python
import functools

import jax
import jax.numpy as jnp
from jax import lax
from jax.experimental import pallas as pl
from jax.experimental.pallas import tpu as pltpu

F32 = jnp.float32
BF16 = jnp.bfloat16

HEAD_DIM = 64
N_HEADS = 8
N_KV = 2
GROUP = N_HEADS // N_KV
N_BRANCH = 3
NSA_WIDTH = N_HEADS * HEAD_DIM
CONV_WIDTH = 512
KV_W = N_KV * HEAD_DIM
ROT_DIM = HEAD_DIM // 4
ROPE_THETA = 500000.0
CMP_LEN = 32
CMP_STRIDE = 16
CMP_HIDDEN = 128
SEL_BLOCK = 64
N_SEL = 8
WINDOW = 512
CONV_K = 3
EPS = 1e-6

LANES = 128
GATE_PAD = 16
NEG = -1e30
LOG2_E = 1.4426950408889634
ONES_ROWS = 16
VMEM_LIMIT = 56 * 1024 * 1024

COL_Q = 0
COL_KV = NSA_WIDTH
COL_CONV = COL_KV + 6 * KV_W
COL_GATE = COL_CONV + 3 * CONV_WIDTH
N_IN_PAD = COL_GATE + N_KV * LANES


def _dot(a, b):
    return jnp.dot(a, b, preferred_element_type=F32)


def _silu(v):
    return v * jax.nn.sigmoid(v)


def _params(*sem):
    return pltpu.CompilerParams(dimension_semantics=sem, vmem_limit_bytes=VMEM_LIMIT)


def _mod_kernel(c_ref, w_ref, b_ref, o_ref):
    c = c_ref[...]
    o_ref[...] = _dot(_silu(c).astype(BF16), w_ref[...].astype(BF16)) + b_ref[...]


def _modulation(c, w_mod, b_mod):
    depth, d, n = w_mod.shape
    b = c.shape[0]
    tn = d
    return pl.pallas_call(
        _mod_kernel,
        grid=(depth, n // tn),
        in_specs=[
            pl.BlockSpec((b, d), lambda l, j: (0, 0)),
            pl.BlockSpec((None, d, tn), lambda l, j: (l, 0, j)),
            pl.BlockSpec((None, 1, tn), lambda l, j: (l, 0, j)),
        ],
        out_specs=pl.BlockSpec((None, b, tn), lambda l, j: (l, 0, j)),
        out_shape=jax.ShapeDtypeStruct((depth, b, n), F32),
        compiler_params=_params("arbitrary", "arbitrary"),
        name="modulation",
    )(c, w_mod, b_mod.reshape(depth, 1, n))


def _rope_kernel(pos_ref, freq_ref, cos_ref, sin_ref):
    ang = pos_ref[0].astype(F32) * freq_ref[...]
    lane = lax.broadcasted_iota(jnp.int32, (1, LANES), 1) % HEAD_DIM
    s = jnp.sin(ang)
    cos_ref[0] = jnp.cos(ang)
    sin_ref[0] = jnp.where(lane < ROT_DIM // 2, -s, s)


def _rope_tables(positions):
    b, s = positions.shape
    half = ROT_DIM // 2
    freqs = ROPE_THETA ** (-jnp.arange(half, dtype=F32) / half)
    lane = jnp.arange(LANES) % HEAD_DIM
    freq_row = jnp.where(lane < ROT_DIM, freqs[lane % half], 0.0).reshape(1, LANES)
    ts = min(s, 1024)
    spec = pl.BlockSpec((1, ts, LANES), lambda i, j: (i, j, 0))
    return pl.pallas_call(
        _rope_kernel,
        grid=(b, s // ts),
        in_specs=[
            pl.BlockSpec((1, ts, 1), lambda i, j: (i, j, 0)),
            pl.BlockSpec((1, LANES), lambda i, j: (0, 0)),
        ],
        out_specs=[spec, spec],
        out_shape=[jax.ShapeDtypeStruct((b, s, LANES), F32)] * 2,
        compiler_params=_params("arbitrary", "arbitrary"),
        name="rope_tables",
    )(positions.reshape(b, s, 1), freq_row)


def _rms_mod(x, g, scale, shift):
    r = lax.rsqrt(jnp.mean(x * x, axis=-1, keepdims=True) + EPS)
    return (x * r) * g * (1.0 + scale) + shift


def _inproj_kernel(x_ref, mod_ref, g_ref, w_ref, cos_ref, sin_ref,
                   q_ref, kv_ref, conv_ref, gate_ref):
    h = _rms_mod(x_ref[0], g_ref[...], mod_ref[0, 1:2, :], mod_ref[0, 0:1, :]).astype(BF16)
    cos_t = cos_ref[0]
    sin_t = sin_ref[0]
    first = (lax.broadcasted_iota(jnp.int32, (1, LANES), 1) % HEAD_DIM) < ROT_DIM // 2

    def rope(y):
        partner = jnp.where(first, pltpu.roll(y, LANES - ROT_DIM // 2, 1),
                            pltpu.roll(y, ROT_DIM // 2, 1))
        return y * cos_t + partner * sin_t

    scale = HEAD_DIM ** -0.5 * LOG2_E
    yq = _dot(h, w_ref[:, COL_Q:COL_KV])
    for j in range(NSA_WIDTH // LANES):
        sl = slice(j * LANES, (j + 1) * LANES)
        q_ref[0, :, sl] = (rope(yq[:, sl]) * scale).astype(BF16)
    ykv = _dot(h, w_ref[:, COL_KV:COL_CONV])
    for j in range(6):
        sl = slice(j * LANES, (j + 1) * LANES)
        part = ykv[:, sl]
        kv_ref[0, :, sl] = (rope(part) if j % 2 == 0 else part).astype(BF16)
    conv_ref[0] = _dot(h, w_ref[:, COL_CONV:COL_GATE])
    gate_ref[0] = _dot(h, w_ref[:, COL_GATE:N_IN_PAD])


def _in_projection(x, mod, g, w, cos_t, sin_t, tm):
    b, s, d = x.shape
    row = lambda n: pl.BlockSpec((1, tm, n), lambda i, j: (i, j, 0))
    return pl.pallas_call(
        _inproj_kernel,
        grid=(b, s // tm),
        in_specs=[
            row(d),
            pl.BlockSpec((1, 6, d), lambda i, j: (i, 0, 0)),
            pl.BlockSpec((1, d), lambda i, j: (0, 0)),
            pl.BlockSpec((d, N_IN_PAD), lambda i, j: (0, 0)),
            row(LANES), row(LANES),
        ],
        out_specs=[row(NSA_WIDTH), row(6 * KV_W), row(3 * CONV_WIDTH), row(N_KV * LANES)],
        out_shape=[
            jax.ShapeDtypeStruct((b, s, NSA_WIDTH), BF16),
            jax.ShapeDtypeStruct((b, s, 6 * KV_W), BF16),
            jax.ShapeDtypeStruct((b, s, 3 * CONV_WIDTH), F32),
            jax.ShapeDtypeStruct((b, s, N_KV * LANES), F32),
        ],
        compiler_params=_params("arbitrary", "arbitrary"),
        name="in_projection",
    )(x, mod, g, w, cos_t, sin_t)


def _compress_kernel(r_ref, pe_ref, w1_ref, w2_ref, o_ref):
    r = r_ref[0, 0, 0].astype(F32)
    n_rows, half = r.shape
    lo = (r + pe_ref[0, 0:1, :]).astype(BF16)
    hi = (r + pe_ref[0, 1:2, :]).astype(BF16)
    y_lo = _dot(lo, w1_ref[0, :half, :])
    y_hi = _dot(hi, w1_ref[0, half:, :])
    pre = y_lo + pltpu.roll(y_hi, n_rows - 1, 0)
    out = _dot(_silu(pre).astype(BF16), w2_ref[0])
    rows = lax.broadcasted_iota(jnp.int32, out.shape, 0)
    o_ref[0, 0] = jnp.where(rows < n_rows - 1, out, 0.0)


def _compress(r, pe, w1, w2):
    b, _, _, n_rows, width = r.shape
    return pl.pallas_call(
        _compress_kernel,
        grid=(b, 2 * N_KV),
        in_specs=[
            pl.BlockSpec((1, 1, 1, n_rows, width), lambda i, j: (i, j // N_KV, j % N_KV, 0, 0)),
            pl.BlockSpec((1, 2, width), lambda i, j: (j // N_KV, 0, 0)),
            pl.BlockSpec((1, 2 * width, CMP_HIDDEN), lambda i, j: (j // N_KV, 0, 0)),
            pl.BlockSpec((1, CMP_HIDDEN, HEAD_DIM), lambda i, j: (j // N_KV, 0, 0)),
        ],
        out_specs=pl.BlockSpec((1, 1, n_rows, HEAD_DIM), lambda i, j: (i, j, 0, 0)),
        out_shape=jax.ShapeDtypeStruct((b, 2 * N_KV, n_rows, HEAD_DIM), F32),
        compiler_params=_params("arbitrary", "arbitrary"),
        name="compress",
    )(r, pe, w1, w2)


def _attn_kernel(qt_ref, kc_ref, vct_ref, ks_ref, vst_ref, kw_ref, vwt_ref, gt_ref, ov_ref,
                 o_ref, selb_ref, acc_s0, acc_s1, acc_w0, acc_w1, sc_s0, sc_s1, sc_w0, sc_w1,
                 *, tq, tk, n_sel):
    t0 = pl.program_id(1) * tq
    rows = GROUP * tq
    t_lane = t0 + lax.broadcasted_iota(jnp.int32, (1, tq), 1)
    acc_sel = (acc_s0, acc_s1)
    acc_win = (acc_w0, acc_w1)
    sc_sel = (sc_s0, sc_s1)
    sc_win = (sc_w0, sc_w1)

    def over_group(m):
        return jnp.concatenate([m] * GROUP, axis=1)

    def compressed_branch(h, qt):
        n_cmp = kc_ref.shape[2]
        s_c = _dot(kc_ref[0, h], qt)
        cmp_end = lax.broadcasted_iota(jnp.int32, (n_cmp, 1), 0) * CMP_STRIDE + (CMP_LEN - 1)
        s_c = jnp.where(cmp_end <= over_group(t_lane), s_c, -jnp.inf)
        m_c = jnp.max(s_c, axis=0, keepdims=True)
        m_c = jnp.where(m_c == -jnp.inf, 0.0, m_c)
        p_c = jnp.exp2(s_c - m_c)
        p_c = p_c / jnp.maximum(jnp.sum(p_c, axis=0, keepdims=True), 1e-30)
        o_c = _dot(vct_ref[0, h], p_c.astype(BF16))
        p_sum = p_c[:, 0:tq]
        for g in range(1, GROUP):
            p_sum = p_sum + p_c[:, g * tq:(g + 1) * tq]
        p_hi = p_sum.astype(BF16)
        p_lo = (p_sum - p_hi.astype(F32)).astype(BF16)
        imp = _dot(ov_ref[...], p_hi) + _dot(ov_ref[...], p_lo)

        n_blk = imp.shape[0]
        j_blk = lax.broadcasted_iota(jnp.int32, (n_blk, tq), 0).astype(F32)
        blk_q = (t_lane // SEL_BLOCK).astype(F32)
        causal = j_blk <= blk_q
        forced = causal & ((j_blk == 0.0) | (j_blk == blk_q) | (j_blk == blk_q - 1.0))
        val = jnp.where(forced, jnp.inf, jnp.where(causal, imp, -jnp.inf))
        bias = jnp.full((n_blk, tq), NEG, F32)
        for _ in range(n_sel):
            top = jnp.max(val, axis=0, keepdims=True)
            cand = (val == top) & (val > -jnp.inf)
            first = jnp.min(jnp.where(cand, j_blk, float(n_blk)), axis=0, keepdims=True)
            pick = j_blk == first
            bias = jnp.where(pick, 0.0, bias)
            val = jnp.where(pick, -jnp.inf, val)
        selb_ref[h] = bias
        return o_c

    qts, o_cs = [], []
    for h in range(N_KV):
        qt = jnp.concatenate([qt_ref[0, h * GROUP + g] for g in range(GROUP)], axis=1)
        qts.append(qt)
        o_cs.append(compressed_branch(h, qt))
        acc_sel[h][...] = jnp.zeros_like(acc_sel[h])
        acc_win[h][...] = jnp.zeros_like(acc_win[h])

    def scores(s_ref, k_tile, qt, mask_bias):
        s_ref[...] = _dot(k_tile, qt) + over_group(mask_bias)

    def flash_step(s_ref, vt_tile, m, acc_ref):
        m_new = jnp.maximum(m, jnp.max(s_ref[...], axis=0, keepdims=True))
        alpha = jnp.exp2(m - m_new)
        p = jnp.exp2(s_ref[...] - m_new).astype(BF16)
        acc_ref[...] = alpha * acc_ref[...] + _dot(vt_tile, p)
        return m_new

    u_sub = lax.broadcasted_iota(jnp.int32, (tk, 1), 0)
    hi_tile = (t0 + tq - 1) // tk
    lo_tile = jnp.maximum(t0 - (WINDOW - 1), 0) // tk
    n_win = hi_tile - lo_tile + 1
    per_blk = tk // SEL_BLOCK

    def block_bias(h, kt):
        return jnp.concatenate(
            [jnp.broadcast_to(selb_ref[h, pl.ds(kt * per_blk + jj, 1), :], (SEL_BLOCK, tq))
             for jj in range(per_blk)], axis=0)

    def near_body(i, ms):
        kt = hi_tile - i
        dist = t_lane - (kt * tk + u_sub)
        causal = dist >= 0
        win_bias = jnp.where(causal & (dist < WINDOW), 0.0, NEG)
        for h in range(N_KV):
            scores(sc_sel[h], ks_ref[0, h, kt], qts[h], jnp.where(causal, block_bias(h, kt), NEG))
        for h in range(N_KV):
            scores(sc_win[h], kw_ref[0, h, kt], qts[h], win_bias)
        out = [flash_step(sc_sel[h], vst_ref[0, h, kt], ms[h], acc_sel[h]) for h in range(N_KV)]
        out += [flash_step(sc_win[h], vwt_ref[0, h, kt], ms[N_KV + h], acc_win[h])
                for h in range(N_KV)]
        return tuple(out)

    def far_body(i, ms):
        kt = hi_tile - i
        for h in range(N_KV):
            scores(sc_sel[h], ks_ref[0, h, kt], qts[h], block_bias(h, kt))
        return tuple(flash_step(sc_sel[h], vst_ref[0, h, kt], ms[h], acc_sel[h])
                     for h in range(N_KV))

    m_init = jnp.full((1, rows), NEG, F32)
    ms = lax.fori_loop(0, n_win, near_body, (m_init,) * (2 * N_KV))
    lax.fori_loop(n_win, hi_tile + 1, far_body, ms[:N_KV])

    for h in range(N_KV):
        a_s = acc_sel[h][...]
        a_w = acc_win[h][...]
        o_s = a_s[:HEAD_DIM] / jnp.maximum(a_s[HEAD_DIM:HEAD_DIM + 1], 1e-30)
        o_w = a_w[:HEAD_DIM] / jnp.maximum(a_w[HEAD_DIM:HEAD_DIM + 1], 1e-30)
        gate = jax.nn.sigmoid(gt_ref[0, h])
        for g in range(GROUP):
            sl = slice(g * tq, (g + 1) * tq)
            o_ref[0, h * GROUP + g] = (gate[g:g + 1] * o_cs[h][:, sl]
                                       + gate[GROUP + g:GROUP + g + 1] * o_s[:, sl]
                                       + gate[2 * GROUP + g:2 * GROUP + g + 1] * o_w[:, sl])


def _attention(qt, kc, vct, ks, vst, kw, vwt, gt, ov, tq, tk):
    b, _, _, s = qt.shape
    n_cmp = kc.shape[2]
    n_kt = s // tk
    n_blk = ov.shape[0]
    n_sel = min(N_SEL, s // SEL_BLOCK)
    v_rows = vst.shape[3]
    kern = functools.partial(_attn_kernel, tq=tq, tk=tk, n_sel=n_sel)
    k_spec = pl.BlockSpec((1, N_KV, n_kt, tk, HEAD_DIM), lambda i, j: (i, 0, 0, 0, 0))
    vt_spec = pl.BlockSpec((1, N_KV, n_kt, v_rows, tk), lambda i, j: (i, 0, 0, 0, 0))
    acc = pltpu.VMEM((v_rows, GROUP * tq), F32)
    return pl.pallas_call(
        kern,
        grid=(b, s // tq),
        in_specs=[
            pl.BlockSpec((1, N_HEADS, HEAD_DIM, tq), lambda i, j: (i, 0, 0, j)),
            pl.BlockSpec((1, N_KV, n_cmp, HEAD_DIM), lambda i, j: (i, 0, 0, 0)),
            pl.BlockSpec((1, N_KV, HEAD_DIM, n_cmp), lambda i, j: (i, 0, 0, 0)),
            k_spec, vt_spec, k_spec, vt_spec,
            pl.BlockSpec((1, N_KV, GATE_PAD, tq), lambda i, j: (i, 0, 0, j)),
            pl.BlockSpec((n_blk, n_cmp), lambda i, j: (0, 0)),
        ],
        out_specs=pl.BlockSpec((1, N_HEADS, HEAD_DIM, tq), lambda i, j: (i, 0, 0, j)),
        out_shape=jax.ShapeDtypeStruct((b, N_HEADS, HEAD_DIM, s), F32),
        scratch_shapes=[pltpu.VMEM((N_KV, n_blk, tq), F32), acc, acc, acc, acc]
        + [pltpu.VMEM((tk, GROUP * tq), F32)] * 4,
        compiler_params=_params("arbitrary", "arbitrary"),
        name="nsa_attention",
    )(qt, kc, vct, ks, vst, kw, vwt, gt, ov)


def _shifted_taps(cur, prev, taps):
    t = cur.shape[0]
    ext = jnp.concatenate([prev, cur], axis=0)
    return taps[0:1] * ext[6:6 + t] + taps[1:2] * ext[7:7 + t] + taps[2:3] * cur


def _group_norm(v, g):
    r = lax.rsqrt(jnp.mean(v * v, axis=-1, keepdims=True) + EPS)
    return (v * r) * g


def _outproj_kernel(attn_ref, cb_ref, cc_ref, ch_ref, ccp_ref, chp_ref, x_ref, mod_ref,
                    ga_ref, gc_ref, cw_ref, w_ref, o_ref):
    z = cc_ref[0] * ch_ref[0]
    z_prev = jnp.where(pl.program_id(1) > 0, ccp_ref[0] * chp_ref[0], 0.0)
    conv = cb_ref[0] * _shifted_taps(z, z_prev, cw_ref[...])
    mix_a = _group_norm(attn_ref[0], ga_ref[...]).astype(BF16)
    mix_c = _group_norm(conv, gc_ref[...]).astype(BF16)
    y = _dot(mix_a, w_ref[:NSA_WIDTH, :]) + _dot(mix_c, w_ref[NSA_WIDTH:, :])
    o_ref[0] = x_ref[0] + mod_ref[0, 2:3, :] * y


def _out_projection(attn, conv3, x, mod, ga, gc, cw, w, tm):
    b, s, d = x.shape
    halo = tm // 8
    col = lambda c: pl.BlockSpec((1, tm, CONV_WIDTH), lambda i, j: (i, j, c))
    prev = lambda c: pl.BlockSpec((1, 8, CONV_WIDTH),
                                  lambda i, j: (i, jnp.maximum(j * halo - 1, 0), c))
    const = lambda shape: pl.BlockSpec(shape, lambda i, j: (0,) * len(shape))
    return pl.pallas_call(
        _outproj_kernel,
        grid=(b, s // tm),
        in_specs=[
            pl.BlockSpec((1, tm, NSA_WIDTH), lambda i, j: (i, j, 0)),
            col(0), col(1), col(2), prev(1), prev(2),
            pl.BlockSpec((1, tm, d), lambda i, j: (i, j, 0)),
            pl.BlockSpec((1, 6, d), lambda i, j: (i, 0, 0)),
            const((1, NSA_WIDTH)), const((1, CONV_WIDTH)), const((CONV_K, CONV_WIDTH)),
            const((NSA_WIDTH + CONV_WIDTH, d)),
        ],
        out_specs=pl.BlockSpec((1, tm, d), lambda i, j: (i, j, 0)),
        out_shape=jax.ShapeDtypeStruct((b, s, d), F32),
        compiler_params=_params("arbitrary", "arbitrary"),
        name="out_projection",
    )(attn, conv3, conv3, conv3, conv3, conv3, x, mod, ga, gc, cw, w)


def _ffn_kernel(x_ref, mod_ref, g_ref, wu_ref, cw_ref, wd_ref, fg_ref, o_ref,
                act_ref, carry_ref, *, d_ff, tf, final_norm):
    x = x_ref[0]
    h = _rms_mod(x, g_ref[...], mod_ref[0, 4:5, :], mod_ref[0, 3:4, :]).astype(BF16)

    @pl.when(pl.program_id(1) == 0)
    def _():
        carry_ref[...] = jnp.zeros_like(carry_ref)

    def conv_cols(cols):
        u = _dot(h, wu_ref[:, cols])
        prev = carry_ref[:, cols]
        carry_ref[:, cols] = u[u.shape[0] - 8:, :]
        return _shifted_taps(u, prev, cw_ref[:, cols])

    for c in range(d_ff // tf):
        gate = conv_cols(slice(c * tf, (c + 1) * tf))
        value = conv_cols(slice(d_ff + c * tf, d_ff + (c + 1) * tf))
        act_ref[:, c * tf:(c + 1) * tf] = (_silu(gate) * value).astype(BF16)
    y = _dot(act_ref[...], wd_ref[...])
    out = x + mod_ref[0, 5:6, :] * y
    if final_norm:
        out = _group_norm(out, fg_ref[...])
    o_ref[0] = out


def _ffn(x, mod, g, wu, cw, wd, fg, tm, final_norm):
    b, s, d = x.shape
    d_ff = wd.shape[0]
    tf = 256
    kern = functools.partial(_ffn_kernel, d_ff=d_ff, tf=tf, final_norm=final_norm)
    const = lambda shape: pl.BlockSpec(shape, lambda i, j: (0,) * len(shape),
                                       pipeline_mode=pl.Buffered(1))
    return pl.pallas_call(
        kern,
        grid=(b, s // tm),
        in_specs=[
            pl.BlockSpec((1, tm, d), lambda i, j: (i, j, 0)),
            pl.BlockSpec((1, 6, d), lambda i, j: (i, 0, 0)),
            const((1, d)), const((d, 2 * d_ff)), const((CONV_K, 2 * d_ff)), const((d_ff, d)),
            const((1, d)),
        ],
        out_specs=pl.BlockSpec((1, tm, d), lambda i, j: (i, j, 0)),
        out_shape=jax.ShapeDtypeStruct((b, s, d), F32),
        scratch_shapes=[pltpu.VMEM((tm, d_ff), BF16), pltpu.VMEM((8, 2 * d_ff), F32)],
        compiler_params=_params("arbitrary", "arbitrary"),
        name="conv_ffn",
    )(x, mod, g, wu, cw, wd, fg)


def _reorder_w_in(w):
    d = w.shape[0]
    o_gate = NSA_WIDTH + 6 * KV_W
    o_conv = o_gate + N_HEADS * N_BRANCH
    gates = w[:, o_gate:o_conv].reshape(d, N_KV, GROUP, N_BRANCH)
    gates = gates.transpose(0, 1, 3, 2).reshape(d, N_KV, N_BRANCH * GROUP)
    gates = jnp.pad(gates, ((0, 0), (0, 0), (0, LANES - N_BRANCH * GROUP)))
    return jnp.concatenate(
        [w[:, :o_gate], w[:, o_conv:], gates.reshape(d, N_KV * LANES)], axis=1).astype(BF16)


def _overlap_matrix(s):
    n_cmp = s // CMP_STRIDE
    n_blk = max(s // SEL_BLOCK, 8)
    cmp_start = jnp.arange(n_cmp) * CMP_STRIDE
    blk_start = jnp.arange(n_blk) * SEL_BLOCK
    ov = ((cmp_start[None, :] < blk_start[:, None] + SEL_BLOCK)
          & (cmp_start[None, :] + CMP_LEN > blk_start[:, None])
          & (jnp.arange(n_cmp)[None, :] < n_cmp - 1))
    return ov.astype(BF16)


def kernel(x, c, positions, w_mod, b_mod, norm1_g, w_in, cmp_pe_k, cmp_w1_k, cmp_w2_k,
           cmp_pe_v, cmp_w1_v, cmp_w2_v, conv_w, grp_g_attn, grp_g_conv, w_out, norm2_g,
           ffn_up, ffn_conv, ffn_down, final_g):
    b, s, d = x.shape
    depth = w_in.shape[0]
    tm = min(512, s)
    tq = 256
    tk = min(512, s)
    n_kt = s // tk
    n_rows = s // CMP_STRIDE

    mod_all = _modulation(c, w_mod, b_mod).reshape(depth, b, 6, d)
    cos_t, sin_t = _rope_tables(positions)
    ov = _overlap_matrix(s)

    for l in range(depth):
        mod = mod_all[l]
        q, kv, conv3, gates = _in_projection(
            x, mod, norm1_g[l].reshape(1, d), _reorder_w_in(w_in[l]), cos_t, sin_t, tm)

        qt = q.reshape(b, s, N_HEADS, HEAD_DIM).transpose(0, 2, 3, 1)
        kv = kv.reshape(b, s, 6, N_KV, HEAD_DIM)
        r = kv[:, :, 0:2].transpose(0, 2, 3, 1, 4).reshape(b, 2, N_KV, n_rows, CMP_STRIDE * HEAD_DIM)
        keys = lambda i: kv[:, :, i].transpose(0, 2, 1, 3).reshape(b, N_KV, n_kt, tk, HEAD_DIM)
        ones = jnp.ones((b, N_KV, n_kt, ONES_ROWS, tk), BF16)
        vals_t = lambda i: jnp.concatenate(
            [kv[:, :, i].reshape(b, n_kt, tk, N_KV, HEAD_DIM).transpose(0, 3, 1, 4, 2), ones], axis=3)
        gt = gates.reshape(b, s, N_KV, LANES)[..., :GATE_PAD].transpose(0, 2, 3, 1)

        half = CMP_LEN // 2
        pe = jnp.stack([cmp_pe_k[l], cmp_pe_v[l]]).reshape(2, 2, half * HEAD_DIM)
        w1 = jnp.stack([cmp_w1_k[l], cmp_w1_v[l]]).astype(BF16)
        w2 = jnp.stack([cmp_w2_k[l], cmp_w2_v[l]]).astype(BF16)
        cmp = _compress(r, pe, w1, w2)
        kc = cmp[:, :N_KV].astype(BF16)
        vct = cmp[:, N_KV:].transpose(0, 1, 3, 2).astype(BF16)

        attn_t = _attention(qt, kc, vct, keys(2), vals_t(3), keys(4), vals_t(5), gt, ov, tq, tk)
        attn = attn_t.transpose(0, 3, 1, 2).reshape(b, s, NSA_WIDTH)

        x = _out_projection(attn, conv3, x, mod, grp_g_attn[l].reshape(1, -1),
                            grp_g_conv[l].reshape(1, -1), conv_w[l], w_out[l].astype(BF16), tm)
        x = _ffn(x, mod, norm2_g[l].reshape(1, d), ffn_up[l].astype(BF16), ffn_conv[l],
                 ffn_down[l].astype(BF16), final_g.reshape(1, d), tm, l == depth - 1)
    return x
```

```python
import functools

import jax
import jax.numpy as jnp
from jax import lax
from jax.experimental import pallas as pl
from jax.experimental.pallas import tpu as pltpu

F32 = jnp.float32
BF16 = jnp.bfloat16

HEAD_DIM = 64
N_HEADS = 8
N_KV = 2
GROUP = N_HEADS // N_KV
N_BRANCH = 3
NSA_WIDTH = N_HEADS * HEAD_DIM
CONV_WIDTH = 512
KV_W = N_KV * HEAD_DIM
ROT_DIM = HEAD_DIM // 4
ROPE_THETA = 500000.0
CMP_LEN = 32
CMP_STRIDE = 16
CMP_HIDDEN = 128
SEL_BLOCK = 64
N_SEL = 8
WINDOW = 512
CONV_K = 3
EPS = 1e-6

LANES = 128
GATE_PAD = 16
NEG = -1e30
LOG2_E = 1.4426950408889634
ONES_ROWS = 16
V_ROWS = HEAD_DIM + ONES_ROWS
VMEM_LIMIT = 56 * 1024 * 1024

COL_Q = 0
COL_KV = NSA_WIDTH
COL_CONV = COL_KV + 6 * KV_W
COL_GATE = COL_CONV + 3 * CONV_WIDTH
N_IN_PAD = COL_GATE + N_KV * LANES


def _dot(a, b):
    return jnp.dot(a, b, preferred_element_type=F32)


def _silu(v):
    return v * jax.nn.sigmoid(v)


def _params(*sem):
    return pltpu.CompilerParams(dimension_semantics=sem, vmem_limit_bytes=VMEM_LIMIT)


def _mod_kernel(c_ref, w_ref, b_ref, o_ref):
    c = c_ref[...]
    o_ref[...] = _dot(_silu(c).astype(BF16), w_ref[...].astype(BF16)) + b_ref[...]


def _modulation(c, w_mod, b_mod):
    depth, d, n = w_mod.shape
    b = c.shape[0]
    tn = d
    return pl.pallas_call(
        _mod_kernel,
        grid=(depth, n // tn),
        in_specs=[
            pl.BlockSpec((b, d), lambda l, j: (0, 0)),
            pl.BlockSpec((None, d, tn), lambda l, j: (l, 0, j)),
            pl.BlockSpec((None, 1, tn), lambda l, j: (l, 0, j)),
        ],
        out_specs=pl.BlockSpec((None, b, tn), lambda l, j: (l, 0, j)),
        out_shape=jax.ShapeDtypeStruct((depth, b, n), F32),
        compiler_params=_params("arbitrary", "arbitrary"),
        name="modulation",
    )(c, w_mod, b_mod.reshape(depth, 1, n))


def _rope_kernel(pos_ref, freq_ref, cos_ref, sin_ref):
    ang = pos_ref[0].astype(F32) * freq_ref[...]
    lane = lax.broadcasted_iota(jnp.int32, (1, LANES), 1) % HEAD_DIM
    s = jnp.sin(ang)
    cos_ref[0] = jnp.cos(ang)
    sin_ref[0] = jnp.where(lane < ROT_DIM // 2, -s, s)


def _rope_tables(positions):
    b, s = positions.shape
    half = ROT_DIM // 2
    freqs = ROPE_THETA ** (-jnp.arange(half, dtype=F32) / half)
    lane = jnp.arange(LANES) % HEAD_DIM
    freq_row = jnp.where(lane < ROT_DIM, freqs[lane % half], 0.0).reshape(1, LANES)
    ts = min(s, 1024)
    spec = pl.BlockSpec((1, ts, LANES), lambda i, j: (i, j, 0))
    return pl.pallas_call(
        _rope_kernel,
        grid=(b, s // ts),
        in_specs=[
            pl.BlockSpec((1, ts, 1), lambda i, j: (i, j, 0)),
            pl.BlockSpec((1, LANES), lambda i, j: (0, 0)),
        ],
        out_specs=[spec, spec],
        out_shape=[jax.ShapeDtypeStruct((b, s, LANES), F32)] * 2,
        compiler_params=_params("arbitrary", "arbitrary"),
        name="rope_tables",
    )(positions.reshape(b, s, 1), freq_row)


def _rms_mod(x, g, scale, shift):
    r = lax.rsqrt(jnp.mean(x * x, axis=-1, keepdims=True) + EPS)
    return (x * r) * g * (1.0 + scale) + shift


def _inproj_kernel(x_ref, mod_ref, g_ref, w_ref, cos_ref, sin_ref,
                   qt_ref, ks_ref, kw_ref, kc_ref, vc_ref, vst_ref, vwt_ref, conv_ref, gt_ref):
    h = _rms_mod(x_ref[0], g_ref[...], mod_ref[0, 1:2, :], mod_ref[0, 0:1, :]).astype(BF16)
    tm = h.shape[0]
    cos_t = cos_ref[0]
    sin_t = sin_ref[0]
    first = (lax.broadcasted_iota(jnp.int32, (1, LANES), 1) % HEAD_DIM) < ROT_DIM // 2

    def rope(y):
        partner = jnp.where(first, pltpu.roll(y, LANES - ROT_DIM // 2, 1),
                            pltpu.roll(y, ROT_DIM // 2, 1))
        return y * cos_t + partner * sin_t

    scale = HEAD_DIM ** -0.5 * LOG2_E
    yq = _dot(h, w_ref[:, COL_Q:COL_KV])
    for j in range(NSA_WIDTH // LANES):
        sl = slice(j * LANES, (j + 1) * LANES)
        qt_ref[0, sl, :] = (rope(yq[:, sl]) * scale).T.astype(BF16)

    ykv = _dot(h, w_ref[:, COL_KV:COL_CONV])
    part = lambda j: ykv[:, j * LANES:(j + 1) * LANES]
    kc_ref[0] = rope(part(0))
    vc_ref[0] = part(1)
    ks_ref[0] = rope(part(2)).astype(BF16)
    kw_ref[0] = rope(part(4)).astype(BF16)
    ones = jnp.ones((ONES_ROWS, tm), BF16)
    for vt_ref, j in ((vst_ref, 3), (vwt_ref, 5)):
        vt = part(j).T
        for hd in range(N_KV):
            vt_ref[0, 0, hd, :HEAD_DIM, :] = vt[hd * HEAD_DIM:(hd + 1) * HEAD_DIM].astype(BF16)
            vt_ref[0, 0, hd, HEAD_DIM:, :] = ones

    conv_ref[0] = _dot(h, w_ref[:, COL_CONV:COL_GATE])
    yg = _dot(h, w_ref[:, COL_GATE:N_IN_PAD])
    for hd in range(N_KV):
        gt = yg[:, hd * LANES:(hd + 1) * LANES].T
        gt_ref[0, hd] = jax.nn.sigmoid(gt[:GATE_PAD])


def _in_projection(x, mod, g, w, cos_t, sin_t, tm):
    b, s, d = x.shape
    n_t = s // tm
    row = lambda n: pl.BlockSpec((1, tm, n), lambda i, j: (i, j, 0))
    vt_spec = pl.BlockSpec((1, 1, N_KV, V_ROWS, tm), lambda i, j: (i, j, 0, 0, 0))
    vt_shape = jax.ShapeDtypeStruct((b, n_t, N_KV, V_ROWS, tm), BF16)
    return pl.pallas_call(
        _inproj_kernel,
        grid=(b, n_t),
        in_specs=[
            row(d),
            pl.BlockSpec((1, 6, d), lambda i, j: (i, 0, 0)),
            pl.BlockSpec((1, d), lambda i, j: (0, 0)),
            pl.BlockSpec((d, N_IN_PAD), lambda i, j: (0, 0)),
            row(LANES), row(LANES),
        ],
        out_specs=[
            pl.BlockSpec((1, NSA_WIDTH, tm), lambda i, j: (i, 0, j)),
            row(KV_W), row(KV_W), row(KV_W), row(KV_W), vt_spec, vt_spec, row(3 * CONV_WIDTH),
            pl.BlockSpec((1, N_KV, GATE_PAD, tm), lambda i, j: (i, 0, 0, j)),
        ],
        out_shape=[
            jax.ShapeDtypeStruct((b, NSA_WIDTH, s), BF16),
            jax.ShapeDtypeStruct((b, s, KV_W), BF16),
            jax.ShapeDtypeStruct((b, s, KV_W), BF16),
            jax.ShapeDtypeStruct((b, s, KV_W), F32),
            jax.ShapeDtypeStruct((b, s, KV_W), F32),
            vt_shape, vt_shape,
            jax.ShapeDtypeStruct((b, s, 3 * CONV_WIDTH), F32),
            jax.ShapeDtypeStruct((b, N_KV, GATE_PAD, s), F32),
        ],
        compiler_params=_params("arbitrary", "arbitrary"),
        name="in_projection",
    )(x, mod, g, w, cos_t, sin_t)


def _compress_kernel(tk_ref, tv_ref, pek_ref, pev_ref, w1k_ref, w1v_ref, w2k_ref, w2v_ref,
                     kc_ref, vct_ref):
    n_rows = tk_ref.shape[1] // CMP_STRIDE
    half = CMP_LEN // 2
    acc = [[None, None], [None, None]]
    for l in range(half):
        for kv, (t_ref, pe_ref, w1_ref) in enumerate(((tk_ref, pek_ref, w1k_ref),
                                                      (tv_ref, pev_ref, w1v_ref))):
            tok = t_ref[0, pl.ds(l, n_rows, stride=CMP_STRIDE), :]
            for part in range(2):
                idx = part * half + l
                y = _dot((tok + pe_ref[idx:idx + 1, :]).astype(BF16), w1_ref[idx])
                acc[kv][part] = y if acc[kv][part] is None else acc[kv][part] + y
    outs = []
    for kv, w2_ref in enumerate((w2k_ref, w2v_ref)):
        pre = acc[kv][0] + pltpu.roll(acc[kv][1], n_rows - 1, 0)
        out = _dot(_silu(pre).astype(BF16), w2_ref[...])
        rows = lax.broadcasted_iota(jnp.int32, out.shape, 0)
        outs.append(jnp.where(rows < n_rows - 1, out, 0.0))
    kc_ref[0] = outs[0].astype(BF16)
    vct_ref[0] = outs[1].T.astype(BF16)


def _compress(kcf, vcf, pek, pev, w1k, w1v, w2k, w2v):
    b, s, _ = kcf.shape
    n_rows = s // CMP_STRIDE
    const = lambda a: pl.BlockSpec(a.shape, lambda i: (0,) * a.ndim)
    return pl.pallas_call(
        _compress_kernel,
        grid=(b,),
        in_specs=[pl.BlockSpec((1, s, KV_W), lambda i: (i, 0, 0)),
                  pl.BlockSpec((1, s, KV_W), lambda i: (i, 0, 0)),
                  const(pek), const(pev), const(w1k), const(w1v), const(w2k), const(w2v)],
        out_specs=[pl.BlockSpec((1, n_rows, KV_W), lambda i: (i, 0, 0)),
                   pl.BlockSpec((1, KV_W, n_rows), lambda i: (i, 0, 0))],
        out_shape=[jax.ShapeDtypeStruct((b, n_rows, KV_W), BF16),
                   jax.ShapeDtypeStruct((b, KV_W, n_rows), BF16)],
        compiler_params=_params("arbitrary"),
        name="compress",
    )(kcf, vcf, pek, pev, w1k, w1v, w2k, w2v)


def _attn_kernel(qt_ref, kc_ref, vct_ref, ks_ref, vst_ref, kw_ref, vwt_ref, gt_ref, ov_ref,
                 o_ref, selb_ref, acc_s0, acc_s1, acc_w0, acc_w1, sc_s0, sc_s1, sc_w0, sc_w1,
                 *, tq, tk, n_sel):
    t0 = pl.program_id(1) * tq
    rows = GROUP * tq
    t_lane = t0 + lax.broadcasted_iota(jnp.int32, (1, tq), 1)
    acc_sel = (acc_s0, acc_s1)
    acc_win = (acc_w0, acc_w1)
    sc_sel = (sc_s0, sc_s1)
    sc_win = (sc_w0, sc_w1)

    def over_group(m):
        return jnp.concatenate([m] * GROUP, axis=1)

    def compressed_branch(h, qt):
        n_cmp = kc_ref.shape[1]
        s_c = _dot(kc_ref[0], qt)
        cmp_end = lax.broadcasted_iota(jnp.int32, (n_cmp, 1), 0) * CMP_STRIDE + (CMP_LEN - 1)
        s_c = jnp.where(cmp_end <= over_group(t_lane), s_c, -jnp.inf)
        m_c = jnp.max(s_c, axis=0, keepdims=True)
        m_c = jnp.where(m_c == -jnp.inf, 0.0, m_c)
        p_c = jnp.exp2(s_c - m_c)
        p_c = p_c / jnp.maximum(jnp.sum(p_c, axis=0, keepdims=True), 1e-30)
        o_c = _dot(vct_ref[0, h * HEAD_DIM:(h + 1) * HEAD_DIM, :], p_c.astype(BF16))
        p_sum = p_c[:, 0:tq]
        for g in range(1, GROUP):
            p_sum = p_sum + p_c[:, g * tq:(g + 1) * tq]
        p_hi = p_sum.astype(BF16)
        p_lo = (p_sum - p_hi.astype(F32)).astype(BF16)
        imp = _dot(ov_ref[...], p_hi) + _dot(ov_ref[...], p_lo)

        n_blk = imp.shape[0]
        j_blk = lax.broadcasted_iota(jnp.int32, (n_blk, tq), 0).astype(F32)
        blk_q = (t_lane // SEL_BLOCK).astype(F32)
        causal = j_blk <= blk_q
        forced = causal & ((j_blk == 0.0) | (j_blk == blk_q) | (j_blk == blk_q - 1.0))
        val = jnp.where(forced, jnp.inf, jnp.where(causal, imp, -jnp.inf))
        bias = jnp.full((n_blk, tq), NEG, F32)
        for _ in range(n_sel):
            top = jnp.max(val, axis=0, keepdims=True)
            cand = (val == top) & (val > -jnp.inf)
            first = jnp.min(jnp.where(cand, j_blk, float(n_blk)), axis=0, keepdims=True)
            pick = j_blk == first
            bias = jnp.where(pick, 0.0, bias)
            val = jnp.where(pick, -jnp.inf, val)
        selb_ref[h] = bias
        return o_c

    qts, o_cs = [], []
    zeros = jnp.zeros((HEAD_DIM, rows), BF16)
    for h in range(N_KV):
        qh = jnp.concatenate(
            [qt_ref[0, (h * GROUP + g) * HEAD_DIM:(h * GROUP + g + 1) * HEAD_DIM, :]
             for g in range(GROUP)], axis=1)
        qt = jnp.concatenate([qh, zeros] if h == 0 else [zeros, qh], axis=0)
        qts.append(qt)
        o_cs.append(compressed_branch(h, qt))
        acc_sel[h][...] = jnp.zeros_like(acc_sel[h])
        acc_win[h][...] = jnp.zeros_like(acc_win[h])

    def scores(s_ref, k_tile, qt, mask_bias):
        s_ref[...] = _dot(k_tile, qt) + over_group(mask_bias)

    def flash_step(s_ref, vt_tile, m, acc_ref):
        m_new = jnp.maximum(m, jnp.max(s_ref[...], axis=0, keepdims=True))
        alpha = jnp.exp2(m - m_new)
        p = jnp.exp2(s_ref[...] - m_new).astype(BF16)
        acc_ref[...] = alpha * acc_ref[...] + _dot(vt_tile, p)
        return m_new

    u_sub = lax.broadcasted_iota(jnp.int32, (tk, 1), 0)
    hi_tile = (t0 + tq - 1) // tk
    lo_tile = jnp.maximum(t0 - (WINDOW - 1), 0) // tk
    n_win = hi_tile - lo_tile + 1
    per_blk = tk // SEL_BLOCK

    def block_bias(h, kt):
        return jnp.concatenate(
            [jnp.broadcast_to(selb_ref[h, pl.ds(kt * per_blk + jj, 1), :], (SEL_BLOCK, tq))
             for jj in range(per_blk)], axis=0)

    def k_tile(k_ref, kt):
        return k_ref[0, pl.ds(pl.multiple_of(kt * tk, tk), tk), :]

    def near_body(i, ms):
        kt = hi_tile - i
        dist = t_lane - (kt * tk + u_sub)
        causal = dist >= 0
        win_bias = jnp.where(causal & (dist < WINDOW), 0.0, NEG)
        k_s = k_tile(ks_ref, kt)
        k_w = k_tile(kw_ref, kt)
        for h in range(N_KV):
            scores(sc_sel[h], k_s, qts[h], jnp.where(causal, block_bias(h, kt), NEG))
        for h in range(N_KV):
            scores(sc_win[h], k_w, qts[h], win_bias)
        out = [flash_step(sc_sel[h], vst_ref[0, kt, h], ms[h], acc_sel[h]) for h in range(N_KV)]
        out += [flash_step(sc_win[h], vwt_ref[0, kt, h], ms[N_KV + h], acc_win[h])
                for h in range(N_KV)]
        return tuple(out)

    def far_body(i, ms):
        kt = hi_tile - i
        k_s = k_tile(ks_ref, kt)
        for h in range(N_KV):
            scores(sc_sel[h], k_s, qts[h], block_bias(h, kt))
        return tuple(flash_step(sc_sel[h], vst_ref[0, kt, h], ms[h], acc_sel[h])
                     for h in range(N_KV))

    m_init = jnp.full((1, rows), NEG, F32)
    ms = lax.fori_loop(0, n_win, near_body, (m_init,) * (2 * N_KV))
    lax.fori_loop(n_win, hi_tile + 1, far_body, ms[:N_KV])

    heads = []
    for h in range(N_KV):
        a_s = acc_sel[h][...]
        a_w = acc_win[h][...]
        o_s = a_s[:HEAD_DIM] / jnp.maximum(a_s[HEAD_DIM:HEAD_DIM + 1], 1e-30)
        o_w = a_w[:HEAD_DIM] / jnp.maximum(a_w[HEAD_DIM:HEAD_DIM + 1], 1e-30)
        gate = gt_ref[0, h]
        for g in range(GROUP):
            sl = slice(g * tq, (g + 1) * tq)
            heads.append(gate[g:g + 1] * o_cs[h][:, sl]
                         + gate[GROUP + g:GROUP + g + 1] * o_s[:, sl]
                         + gate[2 * GROUP + g:2 * GROUP + g + 1] * o_w[:, sl])
    for pair in range(N_HEADS // 2):
        both = jnp.concatenate(heads[2 * pair:2 * pair + 2], axis=0)
        o_ref[0, :, pair * LANES:(pair + 1) * LANES] = both.T


def _attention(qt, kc, vct, ks, vst, kw, vwt, gt, ov, tq, tk):
    b, _, s = qt.shape
    n_cmp = kc.shape[1]
    n_kt = s // tk
    n_blk = ov.shape[0]
    n_sel = min(N_SEL, s // SEL_BLOCK)
    kern = functools.partial(_attn_kernel, tq=tq, tk=tk, n_sel=n_sel)
    k_spec = pl.BlockSpec((1, s, KV_W), lambda i, j: (i, 0, 0))
    vt_spec = pl.BlockSpec((1, n_kt, N_KV, V_ROWS, tk), lambda i, j: (i, 0, 0, 0, 0))
    acc = pltpu.VMEM((V_ROWS, GROUP * tq), F32)
    return pl.pallas_call(
        kern,
        grid=(b, s // tq),
        in_specs=[
            pl.BlockSpec((1, NSA_WIDTH, tq), lambda i, j: (i, 0, j)),
            pl.BlockSpec((1, n_cmp, KV_W), lambda i, j: (i, 0, 0)),
            pl.BlockSpec((1, KV_W, n_cmp), lambda i, j: (i, 0, 0)),
            k_spec, vt_spec, k_spec, vt_spec,
            pl.BlockSpec((1, N_KV, GATE_PAD, tq), lambda i, j: (i, 0, 0, j)),
            pl.BlockSpec((n_blk, n_cmp), lambda i, j: (0, 0)),
        ],
        out_specs=pl.BlockSpec((1, tq, NSA_WIDTH), lambda i, j: (i, j, 0)),
        out_shape=jax.ShapeDtypeStruct((b, s, NSA_WIDTH), F32),
        scratch_shapes=[pltpu.VMEM((N_KV, n_blk, tq), F32), acc, acc, acc, acc]
        + [pltpu.VMEM((tk, GROUP * tq), F32)] * 4,
        compiler_params=_params("arbitrary", "arbitrary"),
        name="nsa_attention",
    )(qt, kc, vct, ks, vst, kw, vwt, gt, ov)


def _shifted_taps(cur, prev, taps):
    t = cur.shape[0]
    ext = jnp.concatenate([prev, cur], axis=0)
    return taps[0:1] * ext[6:6 + t] + taps[1:2] * ext[7:7 + t] + taps[2:3] * cur


def _group_norm(v, g):
    r = lax.rsqrt(jnp.mean(v * v, axis=-1, keepdims=True) + EPS)
    return (v * r) * g


def _outproj_kernel(attn_ref, cb_ref, cc_ref, ch_ref, ccp_ref, chp_ref, x_ref, mod_ref,
                    ga_ref, gc_ref, cw_ref, w_ref, o_ref):
    z = cc_ref[0] * ch_ref[0]
    z_prev = jnp.where(pl.program_id(1) > 0, ccp_ref[0] * chp_ref[0], 0.0)
    conv = cb_ref[0] * _shifted_taps(z, z_prev, cw_ref[...])
    mix_a = _group_norm(attn_ref[0], ga_ref[...]).astype(BF16)
    mix_c = _group_norm(conv, gc_ref[...]).astype(BF16)
    y = _dot(mix_a, w_ref[:NSA_WIDTH, :]) + _dot(mix_c, w_ref[NSA_WIDTH:, :])
    o_ref[0] = x_ref[0] + mod_ref[0, 2:3, :] * y


def _out_projection(attn, conv3, x, mod, ga, gc, cw, w, tm):
    b, s, d = x.shape
    halo = tm // 8
    col = lambda c: pl.BlockSpec((1, tm, CONV_WIDTH), lambda i, j: (i, j, c))
    prev = lambda c: pl.BlockSpec((1, 8, CONV_WIDTH),
                                  lambda i, j: (i, jnp.maximum(j * halo - 1, 0), c))
    const = lambda shape: pl.BlockSpec(shape, lambda i, j: (0,) * len(shape))
    return pl.pallas_call(
        _outproj_kernel,
        grid=(b, s // tm),
        in_specs=[
            pl.BlockSpec((1, tm, NSA_WIDTH), lambda i, j: (i, j, 0)),
            col(0), col(1), col(2), prev(1), prev(2),
            pl.BlockSpec((1, tm, d), lambda i, j: (i, j, 0)),
            pl.BlockSpec((1, 6, d), lambda i, j: (i, 0, 0)),
            const((1, NSA_WIDTH)), const((1, CONV_WIDTH)), const((CONV_K, CONV_WIDTH)),
            const((NSA_WIDTH + CONV_WIDTH, d)),
        ],
        out_specs=pl.BlockSpec((1, tm, d), lambda i, j: (i, j, 0)),
        out_shape=jax.ShapeDtypeStruct((b, s, d), F32),
        compiler_params=_params("arbitrary", "arbitrary"),
        name="out_projection",
    )(attn, conv3, conv3, conv3, conv3, conv3, x, mod, ga, gc, cw, w)


def _ffn_kernel(x_ref, mod_ref, g_ref, wu_ref, cw_ref, wd_ref, fg_ref, o_ref,
                act_ref, carry_ref, *, d_ff, tf, final_norm):
    x = x_ref[0]
    h = _rms_mod(x, g_ref[...], mod_ref[0, 4:5, :], mod_ref[0, 3:4, :]).astype(BF16)

    @pl.when(pl.program_id(1) == 0)
    def _():
        carry_ref[...] = jnp.zeros_like(carry_ref)

    def conv_cols(cols):
        u = _dot(h, wu_ref[:, cols])
        prev = carry_ref[:, cols]
        carry_ref[:, cols] = u[u.shape[0] - 8:, :]
        return _shifted_taps(u, prev, cw_ref[:, cols])

    for c in range(d_ff // tf):
        gate = conv_cols(slice(c * tf, (c + 1) * tf))
        value = conv_cols(slice(d_ff + c * tf, d_ff + (c + 1) * tf))
        act_ref[:, c * tf:(c + 1) * tf] = (_silu(gate) * value).astype(BF16)
    y = _dot(act_ref[...], wd_ref[...])
    out = x + mod_ref[0, 5:6, :] * y
    if final_norm:
        out = _group_norm(out, fg_ref[...])
    o_ref[0] = out


def _ffn(x, mod, g, wu, cw, wd, fg, tm, final_norm):
    b, s, d = x.shape
    d_ff = wd.shape[0]
    tf = 256
    kern = functools.partial(_ffn_kernel, d_ff=d_ff, tf=tf, final_norm=final_norm)
    const = lambda shape: pl.BlockSpec(shape, lambda i, j: (0,) * len(shape),
                                       pipeline_mode=pl.Buffered(1))
    return pl.pallas_call(
        kern,
        grid=(b, s // tm),
        in_specs=[
            pl.BlockSpec((1, tm, d), lambda i, j: (i, j, 0)),
            pl.BlockSpec((1, 6, d), lambda i, j: (i, 0, 0)),
            const((1, d)), const((d, 2 * d_ff)), const((CONV_K, 2 * d_ff)), const((d_ff, d)),
            const((1, d)),
        ],
        out_specs=pl.BlockSpec((1, tm, d), lambda i, j: (i, j, 0)),
        out_shape=jax.ShapeDtypeStruct((b, s, d), F32),
        scratch_shapes=[pltpu.VMEM((tm, d_ff), BF16), pltpu.VMEM((8, 2 * d_ff), F32)],
        compiler_params=_params("arbitrary", "arbitrary"),
        name="conv_ffn",
    )(x, mod, g, wu, cw, wd, fg)


def _reorder_w_in(w):
    d = w.shape[0]
    o_gate = NSA_WIDTH + 6 * KV_W
    o_conv = o_gate + N_HEADS * N_BRANCH
    gates = w[:, o_gate:o_conv].reshape(d, N_KV, GROUP, N_BRANCH)
    gates = gates.transpose(0, 1, 3, 2).reshape(d, N_KV, N_BRANCH * GROUP)
    gates = jnp.pad(gates, ((0, 0), (0, 0), (0, LANES - N_BRANCH * GROUP)))
    return jnp.concatenate(
        [w[:, :o_gate], w[:, o_conv:], gates.reshape(d, N_KV * LANES)], axis=1).astype(BF16)


def _two_head_blockdiag(w):
    z = jnp.zeros_like(w)
    return jnp.concatenate([jnp.concatenate([w, z], axis=-1),
                            jnp.concatenate([z, w], axis=-1)], axis=-2)


def _compress_weights(pe, w1, w2):
    pe2 = jnp.concatenate([pe, pe], axis=1)
    w1_2 = _two_head_blockdiag(w1.reshape(CMP_LEN, HEAD_DIM, CMP_HIDDEN)).astype(BF16)
    w2_2 = _two_head_blockdiag(w2).astype(BF16)
    return pe2, w1_2, w2_2


def _overlap_matrix(s):
    n_cmp = s // CMP_STRIDE
    n_blk = max(s // SEL_BLOCK, 8)
    cmp_start = jnp.arange(n_cmp) * CMP_STRIDE
    blk_start = jnp.arange(n_blk) * SEL_BLOCK
    ov = ((cmp_start[None, :] < blk_start[:, None] + SEL_BLOCK)
          & (cmp_start[None, :] + CMP_LEN > blk_start[:, None])
          & (jnp.arange(n_cmp)[None, :] < n_cmp - 1))
    return ov.astype(BF16)


def kernel(x, c, positions, w_mod, b_mod, norm1_g, w_in, cmp_pe_k, cmp_w1_k, cmp_w2_k,
           cmp_pe_v, cmp_w1_v, cmp_w2_v, conv_w, grp_g_attn, grp_g_conv, w_out, norm2_g,
           ffn_up, ffn_conv, ffn_down, final_g):
    b, s, d = x.shape
    depth = w_in.shape[0]
    tm = min(512, s)
    tq = 256

    mod_all = _modulation(c, w_mod, b_mod).reshape(depth, b, 6, d)
    cos_t, sin_t = _rope_tables(positions)
    ov = _overlap_matrix(s)

    for l in range(depth):
        mod = mod_all[l]
        qt, ks, kw, kcf, vcf, vst, vwt, conv3, gt = _in_projection(
            x, mod, norm1_g[l].reshape(1, d), _reorder_w_in(w_in[l]), cos_t, sin_t, tm)
        pek, w1k, w2k = _compress_weights(cmp_pe_k[l], cmp_w1_k[l], cmp_w2_k[l])
        pev, w1v, w2v = _compress_weights(cmp_pe_v[l], cmp_w1_v[l], cmp_w2_v[l])
        kc, vct = _compress(kcf, vcf, pek, pev, w1k, w1v, w2k, w2v)
        attn = _attention(qt, kc, vct, ks, vst, kw, vwt, gt, ov, tq, tm)
        x = _out_projection(attn, conv3, x, mod, grp_g_attn[l].reshape(1, -1),
                            grp_g_conv[l].reshape(1, -1), conv_w[l], w_out[l].astype(BF16), tm)
        x = _ffn(x, mod, norm2_g[l].reshape(1, d), ffn_up[l].astype(BF16), ffn_conv[l],
                 ffn_down[l].astype(BF16), final_g.reshape(1, d), tm, l == depth - 1)
    return x
```

```python
import functools

import jax
import jax.numpy as jnp
from jax import lax
from jax.experimental import pallas as pl
from jax.experimental.pallas import tpu as pltpu

F32 = jnp.float32
BF16 = jnp.bfloat16

HEAD_DIM = 64
N_HEADS = 8
N_KV = 2
GROUP = N_HEADS // N_KV
N_BRANCH = 3
NSA_WIDTH = N_HEADS * HEAD_DIM
CONV_WIDTH = 512
KV_W = N_KV * HEAD_DIM
ROT_DIM = HEAD_DIM // 4
ROPE_THETA = 500000.0
CMP_LEN = 32
CMP_STRIDE = 16
CMP_HIDDEN = 128
SEL_BLOCK = 64
N_SEL = 8
WINDOW = 512
CONV_K = 3
EPS = 1e-6

LANES = 128
GATE_PAD = 16
NEG = -1e30
LOG2_E = 1.4426950408889634
ONES_ROWS = 16
V_ROWS = HEAD_DIM + ONES_ROWS
VMEM_LIMIT = 56 * 1024 * 1024

COL_Q = 0
COL_KV = NSA_WIDTH
COL_CONV = COL_KV + 6 * KV_W
COL_GATE = COL_CONV + 3 * CONV_WIDTH
N_IN_PAD = COL_GATE + N_KV * LANES


def _dot(a, b):
    return jnp.dot(a, b, preferred_element_type=F32)


def _silu(v):
    return v * jax.nn.sigmoid(v)


def _params(*sem):
    return pltpu.CompilerParams(dimension_semantics=sem, vmem_limit_bytes=VMEM_LIMIT)


def _mod_kernel(c_ref, w_ref, b_ref, o_ref):
    c = c_ref[...]
    o_ref[...] = _dot(_silu(c).astype(BF16), w_ref[...].astype(BF16)) + b_ref[...]


def _modulation(c, w_mod, b_mod):
    depth, d, n = w_mod.shape
    b = c.shape[0]
    tn = d
    return pl.pallas_call(
        _mod_kernel,
        grid=(depth, n // tn),
        in_specs=[
            pl.BlockSpec((b, d), lambda l, j: (0, 0)),
            pl.BlockSpec((None, d, tn), lambda l, j: (l, 0, j)),
            pl.BlockSpec((None, 1, tn), lambda l, j: (l, 0, j)),
        ],
        out_specs=pl.BlockSpec((None, b, tn), lambda l, j: (l, 0, j)),
        out_shape=jax.ShapeDtypeStruct((depth, b, n), F32),
        compiler_params=_params("arbitrary", "arbitrary"),
        name="modulation",
    )(c, w_mod, b_mod.reshape(depth, 1, n))


def _rope_kernel(pos_ref, freq_ref, cos_ref, sin_ref):
    ang = pos_ref[0].astype(F32) * freq_ref[...]
    lane = lax.broadcasted_iota(jnp.int32, (1, LANES), 1) % HEAD_DIM
    s = jnp.sin(ang)
    cos_ref[0] = jnp.cos(ang)
    sin_ref[0] = jnp.where(lane < ROT_DIM // 2, -s, s)


def _rope_tables(positions):
    b, s = positions.shape
    half = ROT_DIM // 2
    freqs = ROPE_THETA ** (-jnp.arange(half, dtype=F32) / half)
    lane = jnp.arange(LANES) % HEAD_DIM
    freq_row = jnp.where(lane < ROT_DIM, freqs[lane % half], 0.0).reshape(1, LANES)
    ts = min(s, 1024)
    spec = pl.BlockSpec((1, ts, LANES), lambda i, j: (i, j, 0))
    return pl.pallas_call(
        _rope_kernel,
        grid=(b, s // ts),
        in_specs=[
            pl.BlockSpec((1, ts, 1), lambda i, j: (i, j, 0)),
            pl.BlockSpec((1, LANES), lambda i, j: (0, 0)),
        ],
        out_specs=[spec, spec],
        out_shape=[jax.ShapeDtypeStruct((b, s, LANES), F32)] * 2,
        compiler_params=_params("arbitrary", "arbitrary"),
        name="rope_tables",
    )(positions.reshape(b, s, 1), freq_row)


def _rms_mod(x, g, scale, shift):
    r = lax.rsqrt(jnp.mean(x * x, axis=-1, keepdims=True) + EPS)
    return (x * r) * g * (1.0 + scale) + shift


def _inproj_kernel(x_ref, mod_ref, g_ref, w_ref, cos_ref, sin_ref,
                   qt_ref, ks_ref, kw_ref, kc_ref, vc_ref, vst_ref, vwt_ref, conv_ref, gt_ref):
    h = _rms_mod(x_ref[0], g_ref[...], mod_ref[0, 1:2, :], mod_ref[0, 0:1, :]).astype(BF16)
    tm = h.shape[0]
    cos_t = cos_ref[0]
    sin_t = sin_ref[0]
    first = (lax.broadcasted_iota(jnp.int32, (1, LANES), 1) % HEAD_DIM) < ROT_DIM // 2

    def rope(y):
        partner = jnp.where(first, pltpu.roll(y, LANES - ROT_DIM // 2, 1),
                            pltpu.roll(y, ROT_DIM // 2, 1))
        return y * cos_t + partner * sin_t

    scale = HEAD_DIM ** -0.5 * LOG2_E
    yq = _dot(h, w_ref[:, COL_Q:COL_KV])
    for j in range(NSA_WIDTH // LANES):
        sl = slice(j * LANES, (j + 1) * LANES)
        qt_ref[0, sl, :] = (rope(yq[:, sl]) * scale).T.astype(BF16)

    ykv = _dot(h, w_ref[:, COL_KV:COL_CONV])
    part = lambda j: ykv[:, j * LANES:(j + 1) * LANES]
    kc_ref[0] = rope(part(0))
    vc_ref[0] = part(1)
    ks_ref[0] = rope(part(2)).astype(BF16)
    kw_ref[0] = rope(part(4)).astype(BF16)
    ones = jnp.ones((ONES_ROWS, tm), BF16)
    for vt_ref, j in ((vst_ref, 3), (vwt_ref, 5)):
        vt = part(j).T
        for hd in range(N_KV):
            vt_ref[0, 0, hd, :HEAD_DIM, :] = vt[hd * HEAD_DIM:(hd + 1) * HEAD_DIM].astype(BF16)
            vt_ref[0, 0, hd, HEAD_DIM:, :] = ones

    conv_ref[0] = _dot(h, w_ref[:, COL_CONV:COL_GATE])
    yg = _dot(h, w_ref[:, COL_GATE:N_IN_PAD])
    for hd in range(N_KV):
        gt = yg[:, hd * LANES:(hd + 1) * LANES].T
        gt_ref[0, hd] = jax.nn.sigmoid(gt[:GATE_PAD])


def _in_projection(x, mod, g, w, cos_t, sin_t, tm):
    b, s, d = x.shape
    n_t = s // tm
    row = lambda n: pl.BlockSpec((1, tm, n), lambda i, j: (i, j, 0))
    vt_spec = pl.BlockSpec((1, 1, N_KV, V_ROWS, tm), lambda i, j: (i, j, 0, 0, 0))
    vt_shape = jax.ShapeDtypeStruct((b, n_t, N_KV, V_ROWS, tm), BF16)
    return pl.pallas_call(
        _inproj_kernel,
        grid=(b, n_t),
        in_specs=[
            row(d),
            pl.BlockSpec((1, 6, d), lambda i, j: (i, 0, 0)),
            pl.BlockSpec((1, d), lambda i, j: (0, 0)),
            pl.BlockSpec((d, N_IN_PAD), lambda i, j: (0, 0)),
            row(LANES), row(LANES),
        ],
        out_specs=[
            pl.BlockSpec((1, NSA_WIDTH, tm), lambda i, j: (i, 0, j)),
            row(KV_W), row(KV_W), row(KV_W), row(KV_W), vt_spec, vt_spec, row(3 * CONV_WIDTH),
            pl.BlockSpec((1, N_KV, GATE_PAD, tm), lambda i, j: (i, 0, 0, j)),
        ],
        out_shape=[
            jax.ShapeDtypeStruct((b, NSA_WIDTH, s), BF16),
            jax.ShapeDtypeStruct((b, s, KV_W), BF16),
            jax.ShapeDtypeStruct((b, s, KV_W), BF16),
            jax.ShapeDtypeStruct((b, s, KV_W), F32),
            jax.ShapeDtypeStruct((b, s, KV_W), F32),
            vt_shape, vt_shape,
            jax.ShapeDtypeStruct((b, s, 3 * CONV_WIDTH), F32),
            jax.ShapeDtypeStruct((b, N_KV, GATE_PAD, s), F32),
        ],
        compiler_params=_params("arbitrary", "arbitrary"),
        name="in_projection",
    )(x, mod, g, w, cos_t, sin_t)


def _compress_kernel(tk_ref, tv_ref, pek_ref, pev_ref, w1k_ref, w1v_ref, w2k_ref, w2v_ref,
                     kc_ref, vct_ref):
    n_rows = tk_ref.shape[1] // CMP_STRIDE
    half = CMP_LEN // 2
    acc = [[None, None], [None, None]]
    for l in range(half):
        for kv, (t_ref, pe_ref, w1_ref) in enumerate(((tk_ref, pek_ref, w1k_ref),
                                                      (tv_ref, pev_ref, w1v_ref))):
            tok = t_ref[0, pl.ds(l, n_rows, stride=CMP_STRIDE), :]
            for part in range(2):
                idx = part * half + l
                y = _dot((tok + pe_ref[idx:idx + 1, :]).astype(BF16), w1_ref[idx])
                acc[kv][part] = y if acc[kv][part] is None else acc[kv][part] + y
    outs = []
    for kv, w2_ref in enumerate((w2k_ref, w2v_ref)):
        pre = acc[kv][0] + pltpu.roll(acc[kv][1], n_rows - 1, 0)
        out = _dot(_silu(pre).astype(BF16), w2_ref[...])
        rows = lax.broadcasted_iota(jnp.int32, out.shape, 0)
        outs.append(jnp.where(rows < n_rows - 1, out, 0.0))
    kc_ref[0] = outs[0].astype(BF16)
    vct_ref[0] = outs[1].T.astype(BF16)


def _compress(kcf, vcf, pek, pev, w1k, w1v, w2k, w2v):
    b, s, _ = kcf.shape
    n_rows = s // CMP_STRIDE
    const = lambda a: pl.BlockSpec(a.shape, lambda i: (0,) * a.ndim)
    return pl.pallas_call(
        _compress_kernel,
        grid=(b,),
        in_specs=[pl.BlockSpec((1, s, KV_W), lambda i: (i, 0, 0)),
                  pl.BlockSpec((1, s, KV_W), lambda i: (i, 0, 0)),
                  const(pek), const(pev), const(w1k), const(w1v), const(w2k), const(w2v)],
        out_specs=[pl.BlockSpec((1, n_rows, KV_W), lambda i: (i, 0, 0)),
                   pl.BlockSpec((1, KV_W, n_rows), lambda i: (i, 0, 0))],
        out_shape=[jax.ShapeDtypeStruct((b, n_rows, KV_W), BF16),
                   jax.ShapeDtypeStruct((b, KV_W, n_rows), BF16)],
        compiler_params=_params("arbitrary"),
        name="compress",
    )(kcf, vcf, pek, pev, w1k, w1v, w2k, w2v)


def _attn_kernel(qt_ref, kc_ref, vct_ref, ks_ref, vst_ref, kw_ref, vwt_ref, gt_ref, ov_ref,
                 o_ref, selb_ref, acc_s0, acc_s1, acc_w0, acc_w1, sc_s0, sc_s1, sc_w0, sc_w1,
                 *, tq, tk, n_sel):
    t0 = pl.program_id(1) * tq
    rows = GROUP * tq
    t_lane = t0 + lax.broadcasted_iota(jnp.int32, (1, tq), 1)
    acc_sel = (acc_s0, acc_s1)
    acc_win = (acc_w0, acc_w1)
    sc_sel = (sc_s0, sc_s1)
    sc_win = (sc_w0, sc_w1)

    def over_group(m):
        return jnp.concatenate([m] * GROUP, axis=1)

    def compressed_branch(h, qt):
        n_cmp = kc_ref.shape[1]
        s_c = _dot(kc_ref[0], qt)
        cmp_end = lax.broadcasted_iota(jnp.int32, (n_cmp, 1), 0) * CMP_STRIDE + (CMP_LEN - 1)
        s_c = jnp.where(cmp_end <= over_group(t_lane), s_c, -jnp.inf)
        m_c = jnp.max(s_c, axis=0, keepdims=True)
        m_c = jnp.where(m_c == -jnp.inf, 0.0, m_c)
        p_c = jnp.exp2(s_c - m_c)
        p_c = p_c / jnp.maximum(jnp.sum(p_c, axis=0, keepdims=True), 1e-30)
        o_c = _dot(vct_ref[0, h * HEAD_DIM:(h + 1) * HEAD_DIM, :], p_c.astype(BF16))
        p_sum = p_c[:, 0:tq]
        for g in range(1, GROUP):
            p_sum = p_sum + p_c[:, g * tq:(g + 1) * tq]
        p_hi = p_sum.astype(BF16)
        p_lo = (p_sum - p_hi.astype(F32)).astype(BF16)
        imp = _dot(ov_ref[...], p_hi) + _dot(ov_ref[...], p_lo)

        n_blk = imp.shape[0]
        j_blk = lax.broadcasted_iota(jnp.int32, (n_blk, tq), 0).astype(F32)
        blk_q = (t_lane // SEL_BLOCK).astype(F32)
        causal = j_blk <= blk_q
        forced = causal & ((j_blk == 0.0) | (j_blk == blk_q) | (j_blk == blk_q - 1.0))
        val = jnp.where(forced, jnp.inf, jnp.where(causal, imp, -jnp.inf))
        bias = jnp.full((n_blk, tq), NEG, F32)
        for _ in range(n_sel):
            top = jnp.max(val, axis=0, keepdims=True)
            cand = (val == top) & (val > -jnp.inf)
            first = jnp.min(jnp.where(cand, j_blk, float(n_blk)), axis=0, keepdims=True)
            pick = j_blk == first
            bias = jnp.where(pick, 0.0, bias)
            val = jnp.where(pick, -jnp.inf, val)
        selb_ref[h] = bias
        return o_c

    qts, o_cs = [], []
    zeros = jnp.zeros((HEAD_DIM, rows), BF16)
    for h in range(N_KV):
        qh = jnp.concatenate(
            [qt_ref[0, (h * GROUP + g) * HEAD_DIM:(h * GROUP + g + 1) * HEAD_DIM, :]
             for g in range(GROUP)], axis=1)
        qt = jnp.concatenate([qh, zeros] if h == 0 else [zeros, qh], axis=0)
        qts.append(qt)
        o_cs.append(compressed_branch(h, qt))
        acc_sel[h][...] = jnp.zeros_like(acc_sel[h])
        acc_win[h][...] = jnp.zeros_like(acc_win[h])

    def scores(s_ref, k_tile, qt, mask_bias):
        s = _dot(k_tile, qt) + over_group(mask_bias)
        s_ref[...] = s
        m_tile = jnp.max(s, axis=0, keepdims=True)
        return m_tile, jnp.exp2(s_ref[...] - m_tile).astype(BF16)

    def flash_step(m_tile, p, vt_tile, m, acc_ref):
        m_new = jnp.maximum(m, m_tile)
        alpha = jnp.exp2(m - m_new)
        beta = jnp.exp2(m_tile - m_new)
        acc_ref[...] = alpha * acc_ref[...] + beta * _dot(vt_tile, p)
        return m_new

    u_sub = lax.broadcasted_iota(jnp.int32, (tk, 1), 0)
    hi_tile = (t0 + tq - 1) // tk
    lo_tile = jnp.maximum(t0 - (WINDOW - 1), 0) // tk
    n_win = hi_tile - lo_tile + 1
    per_blk = tk // SEL_BLOCK

    def block_bias(h, kt):
        return jnp.concatenate(
            [jnp.broadcast_to(selb_ref[h, pl.ds(kt * per_blk + jj, 1), :], (SEL_BLOCK, tq))
             for jj in range(per_blk)], axis=0)

    def k_tile(k_ref, kt):
        return k_ref[0, pl.ds(pl.multiple_of(kt * tk, tk), tk), :]

    def near_body(i, ms):
        kt = hi_tile - i
        dist = t_lane - (kt * tk + u_sub)
        causal = dist >= 0
        win_bias = jnp.where(causal & (dist < WINDOW), 0.0, NEG)
        k_s = k_tile(ks_ref, kt)
        k_w = k_tile(kw_ref, kt)
        sel = lambda h: scores(sc_sel[h], k_s, qts[h], jnp.where(causal, block_bias(h, kt), NEG))
        win = lambda h: scores(sc_win[h], k_w, qts[h], win_bias)
        r_s0 = sel(0)
        r_s1 = sel(1)
        m_s0 = flash_step(*r_s0, vst_ref[0, kt, 0], ms[0], acc_sel[0])
        r_w0 = win(0)
        m_s1 = flash_step(*r_s1, vst_ref[0, kt, 1], ms[1], acc_sel[1])
        r_w1 = win(1)
        m_w0 = flash_step(*r_w0, vwt_ref[0, kt, 0], ms[2], acc_win[0])
        m_w1 = flash_step(*r_w1, vwt_ref[0, kt, 1], ms[3], acc_win[1])
        return m_s0, m_s1, m_w0, m_w1

    def far_body(i, ms):
        kt = hi_tile - i
        k_s = k_tile(ks_ref, kt)
        mt_s = [scores(sc_sel[h], k_s, qts[h], block_bias(h, kt)) for h in range(N_KV)]
        return tuple(flash_step(*mt_s[h], vst_ref[0, kt, h], ms[h], acc_sel[h])
                     for h in range(N_KV))

    m_init = jnp.full((1, rows), NEG, F32)
    ms = lax.fori_loop(0, n_win, near_body, (m_init,) * (2 * N_KV))
    lax.fori_loop(n_win, hi_tile + 1, far_body, ms[:N_KV])

    heads = []
    for h in range(N_KV):
        a_s = acc_sel[h][...]
        a_w = acc_win[h][...]
        o_s = a_s[:HEAD_DIM] / jnp.maximum(a_s[HEAD_DIM:HEAD_DIM + 1], 1e-30)
        o_w = a_w[:HEAD_DIM] / jnp.maximum(a_w[HEAD_DIM:HEAD_DIM + 1], 1e-30)
        gate = gt_ref[0, h]
        for g in range(GROUP):
            sl = slice(g * tq, (g + 1) * tq)
            heads.append(gate[g:g + 1] * o_cs[h][:, sl]
                         + gate[GROUP + g:GROUP + g + 1] * o_s[:, sl]
                         + gate[2 * GROUP + g:2 * GROUP + g + 1] * o_w[:, sl])
    for pair in range(N_HEADS // 2):
        both = jnp.concatenate(heads[2 * pair:2 * pair + 2], axis=0)
        o_ref[0, :, pair * LANES:(pair + 1) * LANES] = both.T


def _attention(qt, kc, vct, ks, vst, kw, vwt, gt, ov, tq, tk):
    b, _, s = qt.shape
    n_cmp = kc.shape[1]
    n_kt = s // tk
    n_blk = ov.shape[0]
    n_sel = min(N_SEL, s // SEL_BLOCK)
    kern = functools.partial(_attn_kernel, tq=tq, tk=tk, n_sel=n_sel)
    k_spec = pl.BlockSpec((1, s, KV_W), lambda i, j: (i, 0, 0))
    vt_spec = pl.BlockSpec((1, n_kt, N_KV, V_ROWS, tk), lambda i, j: (i, 0, 0, 0, 0))
    acc = pltpu.VMEM((V_ROWS, GROUP * tq), F32)
    return pl.pallas_call(
        kern,
        grid=(b, s // tq),
        in_specs=[
            pl.BlockSpec((1, NSA_WIDTH, tq), lambda i, j: (i, 0, j)),
            pl.BlockSpec((1, n_cmp, KV_W), lambda i, j: (i, 0, 0)),
            pl.BlockSpec((1, KV_W, n_cmp), lambda i, j: (i, 0, 0)),
            k_spec, vt_spec, k_spec, vt_spec,
            pl.BlockSpec((1, N_KV, GATE_PAD, tq), lambda i, j: (i, 0, 0, j)),
            pl.BlockSpec((n_blk, n_cmp), lambda i, j: (0, 0)),
        ],
        out_specs=pl.BlockSpec((1, tq, NSA_WIDTH), lambda i, j: (i, j, 0)),
        out_shape=jax.ShapeDtypeStruct((b, s, NSA_WIDTH), F32),
        scratch_shapes=[pltpu.VMEM((N_KV, n_blk, tq), F32), acc, acc, acc, acc]
        + [pltpu.VMEM((tk, GROUP * tq), F32)] * 4,
        compiler_params=_params("arbitrary", "arbitrary"),
        name="nsa_attention",
    )(qt, kc, vct, ks, vst, kw, vwt, gt, ov)


def _shifted_taps(cur, prev, taps):
    t = cur.shape[0]
    ext = jnp.concatenate([prev, cur], axis=0)
    return taps[0:1] * ext[6:6 + t] + taps[1:2] * ext[7:7 + t] + taps[2:3] * cur


def _group_norm(v, g):
    r = lax.rsqrt(jnp.mean(v * v, axis=-1, keepdims=True) + EPS)
    return (v * r) * g


def _outproj_kernel(attn_ref, cb_ref, cc_ref, ch_ref, ccp_ref, chp_ref, x_ref, mod_ref,
                    ga_ref, gc_ref, cw_ref, w_ref, o_ref):
    z = cc_ref[0] * ch_ref[0]
    z_prev = jnp.where(pl.program_id(1) > 0, ccp_ref[0] * chp_ref[0], 0.0)
    conv = cb_ref[0] * _shifted_taps(z, z_prev, cw_ref[...])
    mix_a = _group_norm(attn_ref[0], ga_ref[...]).astype(BF16)
    mix_c = _group_norm(conv, gc_ref[...]).astype(BF16)
    y = _dot(mix_a, w_ref[:NSA_WIDTH, :]) + _dot(mix_c, w_ref[NSA_WIDTH:, :])
    o_ref[0] = x_ref[0] + mod_ref[0, 2:3, :] * y


def _out_projection(attn, conv3, x, mod, ga, gc, cw, w, tm):
    b, s, d = x.shape
    halo = tm // 8
    col = lambda c: pl.BlockSpec((1, tm, CONV_WIDTH), lambda i, j: (i, j, c))
    prev = lambda c: pl.BlockSpec((1, 8, CONV_WIDTH),
                                  lambda i, j: (i, jnp.maximum(j * halo - 1, 0), c))
    const = lambda shape: pl.BlockSpec(shape, lambda i, j: (0,) * len(shape))
    return pl.pallas_call(
        _outproj_kernel,
        grid=(b, s // tm),
        in_specs=[
            pl.BlockSpec((1, tm, NSA_WIDTH), lambda i, j: (i, j, 0)),
            col(0), col(1), col(2), prev(1), prev(2),
            pl.BlockSpec((1, tm, d), lambda i, j: (i, j, 0)),
            pl.BlockSpec((1, 6, d), lambda i, j: (i, 0, 0)),
            const((1, NSA_WIDTH)), const((1, CONV_WIDTH)), const((CONV_K, CONV_WIDTH)),
            const((NSA_WIDTH + CONV_WIDTH, d)),
        ],
        out_specs=pl.BlockSpec((1, tm, d), lambda i, j: (i, j, 0)),
        out_shape=jax.ShapeDtypeStruct((b, s, d), F32),
        compiler_params=_params("arbitrary", "arbitrary"),
        name="out_projection",
    )(attn, conv3, conv3, conv3, conv3, conv3, x, mod, ga, gc, cw, w)


def _ffn_kernel(x_ref, mod_ref, g_ref, wu_ref, cw_ref, wd_ref, fg_ref, o_ref,
                act_ref, carry_ref, *, d_ff, tf, final_norm):
    x = x_ref[0]
    h = _rms_mod(x, g_ref[...], mod_ref[0, 4:5, :], mod_ref[0, 3:4, :]).astype(BF16)

    @pl.when(pl.program_id(1) == 0)
    def _():
        carry_ref[...] = jnp.zeros_like(carry_ref)

    def conv_cols(cols):
        u = _dot(h, wu_ref[:, cols])
        prev = carry_ref[:, cols]
        carry_ref[:, cols] = u[u.shape[0] - 8:, :]
        return _shifted_taps(u, prev, cw_ref[:, cols])

    for c in range(d_ff // tf):
        gate = conv_cols(slice(c * tf, (c + 1) * tf))
        value = conv_cols(slice(d_ff + c * tf, d_ff + (c + 1) * tf))
        act_ref[:, c * tf:(c + 1) * tf] = (_silu(gate) * value).astype(BF16)
    y = _dot(act_ref[...], wd_ref[...])
    out = x + mod_ref[0, 5:6, :] * y
    if final_norm:
        out = _group_norm(out, fg_ref[...])
    o_ref[0] = out


def _ffn(x, mod, g, wu, cw, wd, fg, tm, final_norm):
    b, s, d = x.shape
    d_ff = wd.shape[0]
    tf = 256
    kern = functools.partial(_ffn_kernel, d_ff=d_ff, tf=tf, final_norm=final_norm)
    const = lambda shape: pl.BlockSpec(shape, lambda i, j: (0,) * len(shape),
                                       pipeline_mode=pl.Buffered(1))
    return pl.pallas_call(
        kern,
        grid=(b, s // tm),
        in_specs=[
            pl.BlockSpec((1, tm, d), lambda i, j: (i, j, 0)),
            pl.BlockSpec((1, 6, d), lambda i, j: (i, 0, 0)),
            const((1, d)), const((d, 2 * d_ff)), const((CONV_K, 2 * d_ff)), const((d_ff, d)),
            const((1, d)),
        ],
        out_specs=pl.BlockSpec((1, tm, d), lambda i, j: (i, j, 0)),
        out_shape=jax.ShapeDtypeStruct((b, s, d), F32),
        scratch_shapes=[pltpu.VMEM((tm, d_ff), BF16), pltpu.VMEM((8, 2 * d_ff), F32)],
        compiler_params=_params("arbitrary", "arbitrary"),
        name="conv_ffn",
    )(x, mod, g, wu, cw, wd, fg)


def _reorder_w_in(w):
    d = w.shape[0]
    o_gate = NSA_WIDTH + 6 * KV_W
    o_conv = o_gate + N_HEADS * N_BRANCH
    gates = w[:, o_gate:o_conv].reshape(d, N_KV, GROUP, N_BRANCH)
    gates = gates.transpose(0, 1, 3, 2).reshape(d, N_KV, N_BRANCH * GROUP)
    gates = jnp.pad(gates, ((0, 0), (0, 0), (0, LANES - N_BRANCH * GROUP)))
    return jnp.concatenate(
        [w[:, :o_gate], w[:, o_conv:], gates.reshape(d, N_KV * LANES)], axis=1).astype(BF16)


def _two_head_blockdiag(w):
    z = jnp.zeros_like(w)
    return jnp.concatenate([jnp.concatenate([w, z], axis=-1),
                            jnp.concatenate([z, w], axis=-1)], axis=-2)


def _compress_weights(pe, w1, w2):
    pe2 = jnp.concatenate([pe, pe], axis=1)
    w1_2 = _two_head_blockdiag(w1.reshape(CMP_LEN, HEAD_DIM, CMP_HIDDEN)).astype(BF16)
    w2_2 = _two_head_blockdiag(w2).astype(BF16)
    return pe2, w1_2, w2_2


def _overlap_matrix(s):
    n_cmp = s // CMP_STRIDE
    n_blk = max(s // SEL_BLOCK, 8)
    cmp_start = jnp.arange(n_cmp) * CMP_STRIDE
    blk_start = jnp.arange(n_blk) * SEL_BLOCK
    ov = ((cmp_start[None, :] < blk_start[:, None] + SEL_BLOCK)
          & (cmp_start[None, :] + CMP_LEN > blk_start[:, None])
          & (jnp.arange(n_cmp)[None, :] < n_cmp - 1))
    return ov.astype(BF16)


def kernel(x, c, positions, w_mod, b_mod, norm1_g, w_in, cmp_pe_k, cmp_w1_k, cmp_w2_k,
           cmp_pe_v, cmp_w1_v, cmp_w2_v, conv_w, grp_g_attn, grp_g_conv, w_out, norm2_g,
           ffn_up, ffn_conv, ffn_down, final_g):
    b, s, d = x.shape
    depth = w_in.shape[0]
    tm = min(512, s)
    tq = min(512, s)

    mod_all = _modulation(c, w_mod, b_mod).reshape(depth, b, 6, d)
    cos_t, sin_t = _rope_tables(positions)
    ov = _overlap_matrix(s)

    for l in range(depth):
        mod = mod_all[l]
        qt, ks, kw, kcf, vcf, vst, vwt, conv3, gt = _in_projection(
            x, mod, norm1_g[l].reshape(1, d), _reorder_w_in(w_in[l]), cos_t, sin_t, tm)
        pek, w1k, w2k = _compress_weights(cmp_pe_k[l], cmp_w1_k[l], cmp_w2_k[l])
        pev, w1v, w2v = _compress_weights(cmp_pe_v[l], cmp_w1_v[l], cmp_w2_v[l])
        kc, vct = _compress(kcf, vcf, pek, pev, w1k, w1v, w2k, w2v)
        attn = _attention(qt, kc, vct, ks, vst, kw, vwt, gt, ov, tq, tm)
        x = _out_projection(attn, conv3, x, mod, grp_g_attn[l].reshape(1, -1),
                            grp_g_conv[l].reshape(1, -1), conv_w[l], w_out[l].astype(BF16), tm)
        x = _ffn(x, mod, norm2_g[l].reshape(1, d), ffn_up[l].astype(BF16), ffn_conv[l],
                 ffn_down[l].astype(BF16), final_g.reshape(1, d), tm, l == depth - 1)
    return x
```

```python
import functools

import jax
import jax.numpy as jnp
from jax import lax
from jax.experimental import pallas as pl
from jax.experimental.pallas import tpu as pltpu

F32 = jnp.float32
BF16 = jnp.bfloat16

HEAD_DIM = 64
N_HEADS = 8
N_KV = 2
GROUP = N_HEADS // N_KV
N_BRANCH = 3
NSA_WIDTH = N_HEADS * HEAD_DIM
CONV_WIDTH = 512
KV_W = N_KV * HEAD_DIM
ROT_DIM = HEAD_DIM // 4
ROPE_THETA = 500000.0
CMP_LEN = 32
CMP_STRIDE = 16
CMP_HIDDEN = 128
SEL_BLOCK = 64
N_SEL = 8
WINDOW = 512
CONV_K = 3
EPS = 1e-6

LANES = 128
GATE_PAD = 16
NEG = -1e30
LOG2_E = 1.4426950408889634
ONES_ROWS = 16
V_ROWS = HEAD_DIM + ONES_ROWS
HALO_ROWS = 16
VMEM_LIMIT = 56 * 1024 * 1024

COL_Q = 0
COL_KV = NSA_WIDTH
COL_CONV = COL_KV + 6 * KV_W
COL_GATE = COL_CONV + 3 * CONV_WIDTH
N_IN_PAD = COL_GATE + N_KV * LANES


def _dot(a, b):
    return jnp.dot(a, b, preferred_element_type=F32)


def _silu(v):
    return v * jax.nn.sigmoid(v)


def _params(*sem):
    return pltpu.CompilerParams(dimension_semantics=sem, vmem_limit_bytes=VMEM_LIMIT)


def _mod_kernel(c_ref, w_ref, b_ref, o_ref):
    c = c_ref[...]
    o_ref[...] = _dot(_silu(c).astype(BF16), w_ref[...].astype(BF16)) + b_ref[...]


def _modulation(c, w_mod, b_mod):
    depth, d, n = w_mod.shape
    b = c.shape[0]
    tn = d
    return pl.pallas_call(
        _mod_kernel,
        grid=(depth, n // tn),
        in_specs=[
            pl.BlockSpec((b, d), lambda l, j: (0, 0)),
            pl.BlockSpec((None, d, tn), lambda l, j: (l, 0, j)),
            pl.BlockSpec((None, 1, tn), lambda l, j: (l, 0, j)),
        ],
        out_specs=pl.BlockSpec((None, b, tn), lambda l, j: (l, 0, j)),
        out_shape=jax.ShapeDtypeStruct((depth, b, n), F32),
        compiler_params=_params("arbitrary", "arbitrary"),
        name="modulation",
    )(c, w_mod, b_mod.reshape(depth, 1, n))


def _rope_kernel(pos_ref, freq_ref, cos_ref, sin_ref):
    ang = pos_ref[0].astype(F32) * freq_ref[...]
    c = jnp.cos(ang)
    s = jnp.sin(ang)
    rest = (HEAD_DIM - ROT_DIM, ang.shape[1])
    cos_ref[0] = jnp.concatenate([c, c, jnp.ones(rest, F32)] * 2, axis=0).T
    sin_ref[0] = jnp.concatenate([-s, s, jnp.zeros(rest, F32)] * 2, axis=0).T


def _rope_tables(positions):
    b, s = positions.shape
    half = ROT_DIM // 2
    freqs = ROPE_THETA ** (-jnp.arange(half, dtype=F32) / half)
    spec = pl.BlockSpec((1, s, LANES), lambda i: (i, 0, 0))
    return pl.pallas_call(
        _rope_kernel,
        grid=(b,),
        in_specs=[
            pl.BlockSpec((1, 1, s), lambda i: (i, 0, 0)),
            pl.BlockSpec((half, 1), lambda i: (0, 0)),
        ],
        out_specs=[spec, spec],
        out_shape=[jax.ShapeDtypeStruct((b, s, LANES), F32)] * 2,
        compiler_params=_params("arbitrary"),
        name="rope_tables",
    )(positions.reshape(b, 1, s), freqs.reshape(half, 1))


def _rms_mod(x, g, scale, shift):
    r = lax.rsqrt(jnp.mean(x * x, axis=-1, keepdims=True) + EPS)
    return (x * r) * g * (1.0 + scale) + shift


def _inproj_kernel(x_ref, mod_ref, g_ref, w_ref, cos_ref, sin_ref,
                   qt_ref, ks_ref, kw_ref, kc_ref, vc_ref, vst_ref, vwt_ref, conv_ref, gt_ref):
    h = _rms_mod(x_ref[0], g_ref[...], mod_ref[0, 1:2, :], mod_ref[0, 0:1, :]).astype(BF16)
    tm = h.shape[0]
    cos_t = cos_ref[0]
    sin_t = sin_ref[0]
    first = (lax.broadcasted_iota(jnp.int32, (1, LANES), 1) % HEAD_DIM) < ROT_DIM // 2

    def rope(y):
        partner = jnp.where(first, pltpu.roll(y, LANES - ROT_DIM // 2, 1),
                            pltpu.roll(y, ROT_DIM // 2, 1))
        return y * cos_t + partner * sin_t

    scale = HEAD_DIM ** -0.5 * LOG2_E
    yq = _dot(h, w_ref[:, COL_Q:COL_KV])
    for j in range(NSA_WIDTH // LANES):
        sl = slice(j * LANES, (j + 1) * LANES)
        qt_ref[0, sl, :] = (rope(yq[:, sl]) * scale).T.astype(BF16)

    ykv = _dot(h, w_ref[:, COL_KV:COL_CONV])
    part = lambda j: ykv[:, j * LANES:(j + 1) * LANES]
    kc_ref[0] = rope(part(0))
    vc_ref[0] = part(1)
    ks_ref[0] = rope(part(2)).astype(BF16)
    kw_ref[0] = rope(part(4)).astype(BF16)
    ones = jnp.ones((ONES_ROWS, tm), BF16)
    for vt_ref, j in ((vst_ref, 3), (vwt_ref, 5)):
        vt = part(j).T
        for hd in range(N_KV):
            vt_ref[0, 0, hd, :HEAD_DIM, :] = vt[hd * HEAD_DIM:(hd + 1) * HEAD_DIM].astype(BF16)
            vt_ref[0, 0, hd, HEAD_DIM:, :] = ones

    conv_ref[0] = _dot(h, w_ref[:, COL_CONV:COL_GATE]).astype(BF16)
    yg = _dot(h, w_ref[:, COL_GATE:N_IN_PAD])
    for hd in range(N_KV):
        gt = yg[:, hd * LANES:(hd + 1) * LANES].T
        gt_ref[0, hd] = jax.nn.sigmoid(gt[:GATE_PAD])


def _in_projection(x, mod, g, w, cos_t, sin_t, tm):
    b, s, d = x.shape
    n_t = s // tm
    row = lambda n: pl.BlockSpec((1, tm, n), lambda i, j: (i, j, 0))
    vt_spec = pl.BlockSpec((1, 1, N_KV, V_ROWS, tm), lambda i, j: (i, j, 0, 0, 0))
    vt_shape = jax.ShapeDtypeStruct((b, n_t, N_KV, V_ROWS, tm), BF16)
    return pl.pallas_call(
        _inproj_kernel,
        grid=(b, n_t),
        in_specs=[
            row(d),
            pl.BlockSpec((1, 6, d), lambda i, j: (i, 0, 0)),
            pl.BlockSpec((1, d), lambda i, j: (0, 0)),
            pl.BlockSpec((d, N_IN_PAD), lambda i, j: (0, 0)),
            row(LANES), row(LANES),
        ],
        out_specs=[
            pl.BlockSpec((1, NSA_WIDTH, tm), lambda i, j: (i, 0, j)),
            row(KV_W), row(KV_W), row(KV_W), row(KV_W), vt_spec, vt_spec, row(3 * CONV_WIDTH),
            pl.BlockSpec((1, N_KV, GATE_PAD, tm), lambda i, j: (i, 0, 0, j)),
        ],
        out_shape=[
            jax.ShapeDtypeStruct((b, NSA_WIDTH, s), BF16),
            jax.ShapeDtypeStruct((b, s, KV_W), BF16),
            jax.ShapeDtypeStruct((b, s, KV_W), BF16),
            jax.ShapeDtypeStruct((b, s, KV_W), F32),
            jax.ShapeDtypeStruct((b, s, KV_W), F32),
            vt_shape, vt_shape,
            jax.ShapeDtypeStruct((b, s, 3 * CONV_WIDTH), BF16),
            jax.ShapeDtypeStruct((b, N_KV, GATE_PAD, s), F32),
        ],
        compiler_params=_params("arbitrary", "arbitrary"),
        name="in_projection",
    )(x, mod, g, w, cos_t, sin_t)


def _compress_kernel(tk_ref, tv_ref, pek_ref, pev_ref, w1k_ref, w1v_ref, w2k_ref, w2v_ref,
                     kc_ref, vct_ref):
    n_rows = tk_ref.shape[1] // CMP_STRIDE
    half = CMP_LEN // 2
    acc = [[None, None], [None, None]]
    for l in range(half):
        for kv, (t_ref, pe_ref, w1_ref) in enumerate(((tk_ref, pek_ref, w1k_ref),
                                                      (tv_ref, pev_ref, w1v_ref))):
            tok = t_ref[0, pl.ds(l, n_rows, stride=CMP_STRIDE), :]
            for part in range(2):
                idx = part * half + l
                y = _dot((tok + pe_ref[idx:idx + 1, :]).astype(BF16), w1_ref[idx])
                acc[kv][part] = y if acc[kv][part] is None else acc[kv][part] + y
    outs = []
    for kv, w2_ref in enumerate((w2k_ref, w2v_ref)):
        pre = acc[kv][0] + pltpu.roll(acc[kv][1], n_rows - 1, 0)
        out = _dot(_silu(pre).astype(BF16), w2_ref[...])
        rows = lax.broadcasted_iota(jnp.int32, out.shape, 0)
        outs.append(jnp.where(rows < n_rows - 1, out, 0.0))
    kc_ref[0] = outs[0].astype(BF16)
    vct_ref[0] = outs[1].T.astype(BF16)


def _compress(kcf, vcf, pek, pev, w1k, w1v, w2k, w2v):
    b, s, _ = kcf.shape
    n_rows = s // CMP_STRIDE
    const = lambda a: pl.BlockSpec(a.shape, lambda i: (0,) * a.ndim)
    return pl.pallas_call(
        _compress_kernel,
        grid=(b,),
        in_specs=[pl.BlockSpec((1, s, KV_W), lambda i: (i, 0, 0)),
                  pl.BlockSpec((1, s, KV_W), lambda i: (i, 0, 0)),
                  const(pek), const(pev), const(w1k), const(w1v), const(w2k), const(w2v)],
        out_specs=[pl.BlockSpec((1, n_rows, KV_W), lambda i: (i, 0, 0)),
                   pl.BlockSpec((1, KV_W, n_rows), lambda i: (i, 0, 0))],
        out_shape=[jax.ShapeDtypeStruct((b, n_rows, KV_W), BF16),
                   jax.ShapeDtypeStruct((b, KV_W, n_rows), BF16)],
        compiler_params=_params("arbitrary"),
        name="compress",
    )(kcf, vcf, pek, pev, w1k, w1v, w2k, w2v)


def _attn_kernel(qt_ref, kc_ref, vct_ref, ks_ref, vst_ref, kw_ref, vwt_ref, gt_ref, ov_ref,
                 o_ref, selb_ref, acc_s0, acc_s1, acc_w0, acc_w1, sc_s0, sc_s1, sc_w0, sc_w1,
                 *, tq, tk, n_sel):
    t0 = pl.program_id(1) * tq
    rows = GROUP * tq
    t_lane = t0 + lax.broadcasted_iota(jnp.int32, (1, tq), 1)
    acc_sel = (acc_s0, acc_s1)
    acc_win = (acc_w0, acc_w1)
    sc_sel = (sc_s0, sc_s1)
    sc_win = (sc_w0, sc_w1)

    def over_group(m):
        return jnp.concatenate([m] * GROUP, axis=1)

    def compressed_branch(h, qt):
        n_cmp = kc_ref.shape[1]
        s_c = _dot(kc_ref[0], qt)
        cmp_end = lax.broadcasted_iota(jnp.int32, (n_cmp, 1), 0) * CMP_STRIDE + (CMP_LEN - 1)
        s_c = jnp.where(cmp_end <= over_group(t_lane), s_c, -jnp.inf)
        m_c = jnp.max(s_c, axis=0, keepdims=True)
        m_c = jnp.where(m_c == -jnp.inf, 0.0, m_c)
        p_c = jnp.exp2(s_c - m_c)
        p_c = p_c / jnp.maximum(jnp.sum(p_c, axis=0, keepdims=True), 1e-30)
        o_c = _dot(vct_ref[0, h * HEAD_DIM:(h + 1) * HEAD_DIM, :], p_c.astype(BF16))
        p_sum = p_c[:, 0:tq]
        for g in range(1, GROUP):
            p_sum = p_sum + p_c[:, g * tq:(g + 1) * tq]
        p_hi = p_sum.astype(BF16)
        p_lo = (p_sum - p_hi.astype(F32)).astype(BF16)
        imp = _dot(ov_ref[...], p_hi) + _dot(ov_ref[...], p_lo)

        n_blk = imp.shape[0]
        j_blk = lax.broadcasted_iota(jnp.int32, (n_blk, tq), 0).astype(F32)
        blk_q = (t_lane // SEL_BLOCK).astype(F32)
        causal = j_blk <= blk_q
        forced = causal & ((j_blk == 0.0) | (j_blk == blk_q) | (j_blk == blk_q - 1.0))
        val = jnp.where(forced, jnp.inf, jnp.where(causal, imp, -jnp.inf))
        bias = jnp.full((n_blk, tq), NEG, F32)
        for _ in range(n_sel):
            top = jnp.max(val, axis=0, keepdims=True)
            cand = (val == top) & (val > -jnp.inf)
            first = jnp.min(jnp.where(cand, j_blk, float(n_blk)), axis=0, keepdims=True)
            pick = j_blk == first
            bias = jnp.where(pick, 0.0, bias)
            val = jnp.where(pick, -jnp.inf, val)
        selb_ref[h] = bias
        return o_c

    qts, o_cs = [], []
    zeros = jnp.zeros((HEAD_DIM, rows), BF16)
    for h in range(N_KV):
        qh = jnp.concatenate(
            [qt_ref[0, (h * GROUP + g) * HEAD_DIM:(h * GROUP + g + 1) * HEAD_DIM, :]
             for g in range(GROUP)], axis=1)
        qt = jnp.concatenate([qh, zeros] if h == 0 else [zeros, qh], axis=0)
        qts.append(qt)
        o_cs.append(compressed_branch(h, qt))
        acc_sel[h][...] = jnp.zeros_like(acc_sel[h])
        acc_win[h][...] = jnp.zeros_like(acc_win[h])

    def scores(s_ref, k_tile, qt, mask_bias):
        s = _dot(k_tile, qt) + over_group(mask_bias)
        s_ref[...] = s
        m_tile = jnp.max(s, axis=0, keepdims=True)
        return m_tile, jnp.exp2(s_ref[...] - m_tile).astype(BF16)

    def flash_step(m_tile, p, vt_tile, m, acc_ref):
        m_new = jnp.maximum(m, m_tile)
        alpha = jnp.exp2(m - m_new)
        beta = jnp.exp2(m_tile - m_new)
        acc_ref[...] = alpha * acc_ref[...] + beta * _dot(vt_tile, p)
        return m_new

    u_sub = lax.broadcasted_iota(jnp.int32, (tk, 1), 0)
    hi_tile = (t0 + tq - 1) // tk
    lo_tile = jnp.maximum(t0 - (WINDOW - 1), 0) // tk
    n_win = hi_tile - lo_tile + 1
    per_blk = tk // SEL_BLOCK

    def block_bias(h, kt):
        return jnp.concatenate(
            [jnp.broadcast_to(selb_ref[h, pl.ds(kt * per_blk + jj, 1), :], (SEL_BLOCK, tq))
             for jj in range(per_blk)], axis=0)

    def k_tile(k_ref, kt):
        return k_ref[0, pl.ds(pl.multiple_of(kt * tk, tk), tk), :]

    def near_body(i, ms):
        kt = hi_tile - i
        dist = t_lane - (kt * tk + u_sub)
        causal = dist >= 0
        win_bias = jnp.where(causal & (dist < WINDOW), 0.0, NEG)
        k_s = k_tile(ks_ref, kt)
        k_w = k_tile(kw_ref, kt)
        sel = lambda h: scores(sc_sel[h], k_s, qts[h], jnp.where(causal, block_bias(h, kt), NEG))
        win = lambda h: scores(sc_win[h], k_w, qts[h], win_bias)
        r_s0 = sel(0)
        r_s1 = sel(1)
        m_s0 = flash_step(*r_s0, vst_ref[0, kt, 0], ms[0], acc_sel[0])
        r_w0 = win(0)
        m_s1 = flash_step(*r_s1, vst_ref[0, kt, 1], ms[1], acc_sel[1])
        r_w1 = win(1)
        m_w0 = flash_step(*r_w0, vwt_ref[0, kt, 0], ms[2], acc_win[0])
        m_w1 = flash_step(*r_w1, vwt_ref[0, kt, 1], ms[3], acc_win[1])
        return m_s0, m_s1, m_w0, m_w1

    def far_body(i, ms):
        kt = hi_tile - i
        k_s = k_tile(ks_ref, kt)
        mt_s = [scores(sc_sel[h], k_s, qts[h], block_bias(h, kt)) for h in range(N_KV)]
        return tuple(flash_step(*mt_s[h], vst_ref[0, kt, h], ms[h], acc_sel[h])
                     for h in range(N_KV))

    m_init = jnp.full((1, rows), NEG, F32)
    ms = lax.fori_loop(0, n_win, near_body, (m_init,) * (2 * N_KV))
    lax.fori_loop(n_win, hi_tile + 1, far_body, ms[:N_KV])

    heads = []
    for h in range(N_KV):
        a_s = acc_sel[h][...]
        a_w = acc_win[h][...]
        o_s = a_s[:HEAD_DIM] / jnp.maximum(a_s[HEAD_DIM:HEAD_DIM + 1], 1e-30)
        o_w = a_w[:HEAD_DIM] / jnp.maximum(a_w[HEAD_DIM:HEAD_DIM + 1], 1e-30)
        gate = gt_ref[0, h]
        for g in range(GROUP):
            sl = slice(g * tq, (g + 1) * tq)
            heads.append(gate[g:g + 1] * o_cs[h][:, sl]
                         + gate[GROUP + g:GROUP + g + 1] * o_s[:, sl]
                         + gate[2 * GROUP + g:2 * GROUP + g + 1] * o_w[:, sl])
    for pair in range(N_HEADS // 2):
        both = jnp.concatenate(heads[2 * pair:2 * pair + 2], axis=0)
        o_ref[0, :, pair * LANES:(pair + 1) * LANES] = both.T.astype(BF16)


def _attention(qt, kc, vct, ks, vst, kw, vwt, gt, ov, tq, tk):
    b, _, s = qt.shape
    n_cmp = kc.shape[1]
    n_kt = s // tk
    n_blk = ov.shape[0]
    n_sel = min(N_SEL, s // SEL_BLOCK)
    kern = functools.partial(_attn_kernel, tq=tq, tk=tk, n_sel=n_sel)
    k_spec = pl.BlockSpec((1, s, KV_W), lambda i, j: (i, 0, 0))
    vt_spec = pl.BlockSpec((1, n_kt, N_KV, V_ROWS, tk), lambda i, j: (i, 0, 0, 0, 0))
    acc = pltpu.VMEM((V_ROWS, GROUP * tq), F32)
    return pl.pallas_call(
        kern,
        grid=(b, s // tq),
        in_specs=[
            pl.BlockSpec((1, NSA_WIDTH, tq), lambda i, j: (i, 0, j)),
            pl.BlockSpec((1, n_cmp, KV_W), lambda i, j: (i, 0, 0)),
            pl.BlockSpec((1, KV_W, n_cmp), lambda i, j: (i, 0, 0)),
            k_spec, vt_spec, k_spec, vt_spec,
            pl.BlockSpec((1, N_KV, GATE_PAD, tq), lambda i, j: (i, 0, 0, j)),
            pl.BlockSpec((n_blk, n_cmp), lambda i, j: (0, 0)),
        ],
        out_specs=pl.BlockSpec((1, tq, NSA_WIDTH), lambda i, j: (i, j, 0)),
        out_shape=jax.ShapeDtypeStruct((b, s, NSA_WIDTH), BF16),
        scratch_shapes=[pltpu.VMEM((N_KV, n_blk, tq), F32), acc, acc, acc, acc]
        + [pltpu.VMEM((tk, GROUP * tq), F32)] * 4,
        compiler_params=_params("arbitrary", "arbitrary"),
        name="nsa_attention",
    )(qt, kc, vct, ks, vst, kw, vwt, gt, ov)


def _shifted_taps(cur, prev, taps):
    t = cur.shape[0]
    ext = jnp.concatenate([prev, cur], axis=0)
    return taps[0:1] * ext[6:6 + t] + taps[1:2] * ext[7:7 + t] + taps[2:3] * cur


def _group_norm(v, g):
    r = lax.rsqrt(jnp.mean(v * v, axis=-1, keepdims=True) + EPS)
    return (v * r) * g


def _outproj_kernel(attn_ref, cb_ref, cc_ref, ch_ref, ccp_ref, chp_ref, x_ref, mod_ref,
                    ga_ref, gc_ref, cw_ref, w_ref, o_ref):
    f32 = lambda ref: ref[0].astype(F32)
    z = f32(cc_ref) * f32(ch_ref)
    z_halo = f32(ccp_ref) * f32(chp_ref)
    z_prev = jnp.where(pl.program_id(1) > 0, z_halo[8:], 0.0)
    conv = f32(cb_ref) * _shifted_taps(z, z_prev, cw_ref[...])
    mix_a = _group_norm(f32(attn_ref), ga_ref[...]).astype(BF16)
    mix_c = _group_norm(conv, gc_ref[...]).astype(BF16)
    y = _dot(mix_a, w_ref[:NSA_WIDTH, :]) + _dot(mix_c, w_ref[NSA_WIDTH:, :])
    o_ref[0] = x_ref[0] + mod_ref[0, 2:3, :] * y


def _out_projection(attn, conv3, x, mod, ga, gc, cw, w, tm):
    b, s, d = x.shape
    halo = tm // HALO_ROWS
    col = lambda c: pl.BlockSpec((1, tm, CONV_WIDTH), lambda i, j: (i, j, c))
    prev = lambda c: pl.BlockSpec((1, HALO_ROWS, CONV_WIDTH),
                                  lambda i, j: (i, jnp.maximum(j * halo - 1, 0), c))
    const = lambda shape: pl.BlockSpec(shape, lambda i, j: (0,) * len(shape))
    return pl.pallas_call(
        _outproj_kernel,
        grid=(b, s // tm),
        in_specs=[
            pl.BlockSpec((1, tm, NSA_WIDTH), lambda i, j: (i, j, 0)),
            col(0), col(1), col(2), prev(1), prev(2),
            pl.BlockSpec((1, tm, d), lambda i, j: (i, j, 0)),
            pl.BlockSpec((1, 6, d), lambda i, j: (i, 0, 0)),
            const((1, NSA_WIDTH)), const((1, CONV_WIDTH)), const((CONV_K, CONV_WIDTH)),
            const((NSA_WIDTH + CONV_WIDTH, d)),
        ],
        out_specs=pl.BlockSpec((1, tm, d), lambda i, j: (i, j, 0)),
        out_shape=jax.ShapeDtypeStruct((b, s, d), F32),
        compiler_params=_params("arbitrary", "arbitrary"),
        name="out_projection",
    )(attn, conv3, conv3, conv3, conv3, conv3, x, mod, ga, gc, cw, w)


def _ffn_kernel(x_ref, mod_ref, g_ref, wu_ref, cw_ref, wd_ref, fg_ref, o_ref,
                act_ref, carry_ref, *, d_ff, tf, final_norm):
    x = x_ref[0]
    h = _rms_mod(x, g_ref[...], mod_ref[0, 4:5, :], mod_ref[0, 3:4, :]).astype(BF16)

    @pl.when(pl.program_id(1) == 0)
    def _():
        carry_ref[...] = jnp.zeros_like(carry_ref)

    def conv_cols(cols):
        u = _dot(h, wu_ref[:, cols])
        prev = carry_ref[:, cols]
        carry_ref[:, cols] = u[u.shape[0] - 8:, :]
        return _shifted_taps(u, prev, cw_ref[:, cols])

    for c in range(d_ff // tf):
        gate = conv_cols(slice(c * tf, (c + 1) * tf))
        value = conv_cols(slice(d_ff + c * tf, d_ff + (c + 1) * tf))
        act_ref[:, c * tf:(c + 1) * tf] = (_silu(gate) * value).astype(BF16)
    y = _dot(act_ref[...], wd_ref[...])
    out = x + mod_ref[0, 5:6, :] * y
    if final_norm:
        out = _group_norm(out, fg_ref[...])
    o_ref[0] = out


def _ffn(x, mod, g, wu, cw, wd, fg, tm, final_norm):
    b, s, d = x.shape
    d_ff = wd.shape[0]
    tf = 256
    kern = functools.partial(_ffn_kernel, d_ff=d_ff, tf=tf, final_norm=final_norm)
    const = lambda shape: pl.BlockSpec(shape, lambda i, j: (0,) * len(shape),
                                       pipeline_mode=pl.Buffered(1))
    return pl.pallas_call(
        kern,
        grid=(b, s // tm),
        in_specs=[
            pl.BlockSpec((1, tm, d), lambda i, j: (i, j, 0)),
            pl.BlockSpec((1, 6, d), lambda i, j: (i, 0, 0)),
            const((1, d)), const((d, 2 * d_ff)), const((CONV_K, 2 * d_ff)), const((d_ff, d)),
            const((1, d)),
        ],
        out_specs=pl.BlockSpec((1, tm, d), lambda i, j: (i, j, 0)),
        out_shape=jax.ShapeDtypeStruct((b, s, d), F32),
        scratch_shapes=[pltpu.VMEM((tm, d_ff), BF16), pltpu.VMEM((8, 2 * d_ff), F32)],
        compiler_params=_params("arbitrary", "arbitrary"),
        name="conv_ffn",
    )(x, mod, g, wu, cw, wd, fg)


def _reorder_w_in(w):
    d = w.shape[0]
    o_gate = NSA_WIDTH + 6 * KV_W
    o_conv = o_gate + N_HEADS * N_BRANCH
    gates = w[:, o_gate:o_conv].reshape(d, N_KV, GROUP, N_BRANCH)
    gates = gates.transpose(0, 1, 3, 2).reshape(d, N_KV, N_BRANCH * GROUP)
    gates = jnp.pad(gates, ((0, 0), (0, 0), (0, LANES - N_BRANCH * GROUP)))
    return jnp.concatenate(
        [w[:, :o_gate], w[:, o_conv:], gates.reshape(d, N_KV * LANES)], axis=1).astype(BF16)


def _two_head_blockdiag(w):
    z = jnp.zeros_like(w)
    return jnp.concatenate([jnp.concatenate([w, z], axis=-1),
                            jnp.concatenate([z, w], axis=-1)], axis=-2)


def _compress_weights(pe, w1, w2):
    pe2 = jnp.concatenate([pe, pe], axis=1)
    w1_2 = _two_head_blockdiag(w1.reshape(CMP_LEN, HEAD_DIM, CMP_HIDDEN)).astype(BF16)
    w2_2 = _two_head_blockdiag(w2).astype(BF16)
    return pe2, w1_2, w2_2


def _overlap_matrix(s):
    n_cmp = s // CMP_STRIDE
    n_blk = max(s // SEL_BLOCK, 8)
    cmp_start = jnp.arange(n_cmp) * CMP_STRIDE
    blk_start = jnp.arange(n_blk) * SEL_BLOCK
    ov = ((cmp_start[None, :] < blk_start[:, None] + SEL_BLOCK)
          & (cmp_start[None, :] + CMP_LEN > blk_start[:, None])
          & (jnp.arange(n_cmp)[None, :] < n_cmp - 1))
    return ov.astype(BF16)


def kernel(x, c, positions, w_mod, b_mod, norm1_g, w_in, cmp_pe_k, cmp_w1_k, cmp_w2_k,
           cmp_pe_v, cmp_w1_v, cmp_w2_v, conv_w, grp_g_attn, grp_g_conv, w_out, norm2_g,
           ffn_up, ffn_conv, ffn_down, final_g):
    b, s, d = x.shape
    depth = w_in.shape[0]
    tm = min(512, s)
    tq = min(512, s)

    mod_all = _modulation(c, w_mod, b_mod).reshape(depth, b, 6, d)
    cos_t, sin_t = _rope_tables(positions)
    ov = _overlap_matrix(s)

    for l in range(depth):
        mod = mod_all[l]
        qt, ks, kw, kcf, vcf, vst, vwt, conv3, gt = _in_projection(
            x, mod, norm1_g[l].reshape(1, d), _reorder_w_in(w_in[l]), cos_t, sin_t, tm)
        pek, w1k, w2k = _compress_weights(cmp_pe_k[l], cmp_w1_k[l], cmp_w2_k[l])
        pev, w1v, w2v = _compress_weights(cmp_pe_v[l], cmp_w1_v[l], cmp_w2_v[l])
        kc, vct = _compress(kcf, vcf, pek, pev, w1k, w1v, w2k, w2v)
        attn = _attention(qt, kc, vct, ks, vst, kw, vwt, gt, ov, tq, tm)
        x = _out_projection(attn, conv3, x, mod, grp_g_attn[l].reshape(1, -1),
                            grp_g_conv[l].reshape(1, -1), conv_w[l], w_out[l].astype(BF16), tm)
        x = _ffn(x, mod, norm2_g[l].reshape(1, d), ffn_up[l].astype(BF16), ffn_conv[l],
                 ffn_down[l].astype(BF16), final_g.reshape(1, d), tm, l == depth - 1)
    return x
```

```python
import functools

import jax
import jax.numpy as jnp
from jax import lax
from jax.experimental import pallas as pl
from jax.experimental.pallas import tpu as pltpu

F32 = jnp.float32
BF16 = jnp.bfloat16

HEAD_DIM = 64
N_HEADS = 8
N_KV = 2
GROUP = N_HEADS // N_KV
N_BRANCH = 3
NSA_WIDTH = N_HEADS * HEAD_DIM
CONV_WIDTH = 512
KV_W = N_KV * HEAD_DIM
ROT_DIM = HEAD_DIM // 4
ROPE_THETA = 500000.0
CMP_LEN = 32
CMP_STRIDE = 16
CMP_HIDDEN = 128
SEL_BLOCK = 64
N_SEL = 8
WINDOW = 512
CONV_K = 3
EPS = 1e-6

LANES = 128
GATE_PAD = 16
NEG = -1e30
LOG2_E = 1.4426950408889634
ONES_ROWS = 16
V_ROWS = HEAD_DIM + ONES_ROWS
HALO_ROWS = 16
VMEM_LIMIT = 56 * 1024 * 1024

COL_Q = 0
COL_KV = NSA_WIDTH
COL_CONV = COL_KV + 6 * KV_W
COL_GATE = COL_CONV + 3 * CONV_WIDTH
N_IN_PAD = COL_GATE + N_KV * LANES


def _dot(a, b):
    return jnp.dot(a, b, preferred_element_type=F32)


def _silu(v):
    return v * jax.nn.sigmoid(v)


def _params(*sem):
    return pltpu.CompilerParams(dimension_semantics=sem, vmem_limit_bytes=VMEM_LIMIT)


def _mod_kernel(c_ref, w_ref, b_ref, o_ref):
    c = c_ref[...]
    o_ref[...] = _dot(_silu(c).astype(BF16), w_ref[...].astype(BF16)) + b_ref[...]


def _modulation(c, w_mod, b_mod):
    depth, d, n = w_mod.shape
    b = c.shape[0]
    tn = d
    return pl.pallas_call(
        _mod_kernel,
        grid=(depth, n // tn),
        in_specs=[
            pl.BlockSpec((b, d), lambda l, j: (0, 0)),
            pl.BlockSpec((None, d, tn), lambda l, j: (l, 0, j)),
            pl.BlockSpec((None, 1, tn), lambda l, j: (l, 0, j)),
        ],
        out_specs=pl.BlockSpec((None, b, tn), lambda l, j: (l, 0, j)),
        out_shape=jax.ShapeDtypeStruct((depth, b, n), F32),
        compiler_params=_params("arbitrary", "arbitrary"),
        name="modulation",
    )(c, w_mod, b_mod.reshape(depth, 1, n))


def _rope_kernel(pos_ref, freq_ref, cos_ref, sin_ref):
    ang = pos_ref[0].astype(F32) * freq_ref[...]
    c = jnp.cos(ang)
    s = jnp.sin(ang)
    rest = (HEAD_DIM - ROT_DIM, ang.shape[1])
    cos_ref[0] = jnp.concatenate([c, c, jnp.ones(rest, F32)] * 2, axis=0).T
    sin_ref[0] = jnp.concatenate([-s, s, jnp.zeros(rest, F32)] * 2, axis=0).T


def _rope_tables(positions):
    b, s = positions.shape
    half = ROT_DIM // 2
    freqs = ROPE_THETA ** (-jnp.arange(half, dtype=F32) / half)
    spec = pl.BlockSpec((1, s, LANES), lambda i: (i, 0, 0))
    return pl.pallas_call(
        _rope_kernel,
        grid=(b,),
        in_specs=[
            pl.BlockSpec((1, 1, s), lambda i: (i, 0, 0)),
            pl.BlockSpec((half, 1), lambda i: (0, 0)),
        ],
        out_specs=[spec, spec],
        out_shape=[jax.ShapeDtypeStruct((b, s, LANES), F32)] * 2,
        compiler_params=_params("arbitrary"),
        name="rope_tables",
    )(positions.reshape(b, 1, s), freqs.reshape(half, 1))


def _rms_mod(x, g, scale, shift):
    r = lax.rsqrt(jnp.mean(x * x, axis=-1, keepdims=True) + EPS)
    return (x * r) * g * (1.0 + scale) + shift


def _inproj_kernel(x_ref, mod_ref, g_ref, w_ref, cos_ref, sin_ref,
                   qt_ref, ks_ref, kw_ref, kc_ref, vc_ref, vst_ref, vwt_ref, conv_ref, gt_ref):
    h = _rms_mod(x_ref[0], g_ref[...], mod_ref[0, 1:2, :], mod_ref[0, 0:1, :]).astype(BF16)
    tm = h.shape[0]
    cos_t = cos_ref[0]
    sin_t = sin_ref[0]
    first = (lax.broadcasted_iota(jnp.int32, (1, LANES), 1) % HEAD_DIM) < ROT_DIM // 2

    def rope(y):
        partner = jnp.where(first, pltpu.roll(y, LANES - ROT_DIM // 2, 1),
                            pltpu.roll(y, ROT_DIM // 2, 1))
        return y * cos_t + partner * sin_t

    scale = HEAD_DIM ** -0.5 * LOG2_E
    yq = _dot(h, w_ref[:, COL_Q:COL_KV])
    for j in range(NSA_WIDTH // LANES):
        sl = slice(j * LANES, (j + 1) * LANES)
        qt_ref[0, sl, :] = (rope(yq[:, sl]) * scale).T.astype(BF16)

    ykv = _dot(h, w_ref[:, COL_KV:COL_CONV])
    part = lambda j: ykv[:, j * LANES:(j + 1) * LANES]
    kc_ref[0] = rope(part(0))
    vc_ref[0] = part(1)
    ks_ref[0] = rope(part(2)).astype(BF16)
    kw_ref[0] = rope(part(4)).astype(BF16)
    ones = jnp.ones((ONES_ROWS, tm), BF16)
    for vt_ref, j in ((vst_ref, 3), (vwt_ref, 5)):
        vt = part(j).T
        for hd in range(N_KV):
            vt_ref[0, 0, hd, :HEAD_DIM, :] = vt[hd * HEAD_DIM:(hd + 1) * HEAD_DIM].astype(BF16)
            vt_ref[0, 0, hd, HEAD_DIM:, :] = ones

    conv_ref[0] = _dot(h, w_ref[:, COL_CONV:COL_GATE]).astype(BF16)
    yg = _dot(h, w_ref[:, COL_GATE:N_IN_PAD])
    for hd in range(N_KV):
        gt = yg[:, hd * LANES:(hd + 1) * LANES].T
        gt_ref[0, hd] = jax.nn.sigmoid(gt[:GATE_PAD])


def _in_projection(x, mod, g, w, cos_t, sin_t, tm):
    b, s, d = x.shape
    n_t = s // tm
    row = lambda n: pl.BlockSpec((1, tm, n), lambda i, j: (i, j, 0))
    vt_spec = pl.BlockSpec((1, 1, N_KV, V_ROWS, tm), lambda i, j: (i, j, 0, 0, 0))
    vt_shape = jax.ShapeDtypeStruct((b, n_t, N_KV, V_ROWS, tm), BF16)
    return pl.pallas_call(
        _inproj_kernel,
        grid=(b, n_t),
        in_specs=[
            row(d),
            pl.BlockSpec((1, 6, d), lambda i, j: (i, 0, 0)),
            pl.BlockSpec((1, d), lambda i, j: (0, 0)),
            pl.BlockSpec((d, N_IN_PAD), lambda i, j: (0, 0)),
            row(LANES), row(LANES),
        ],
        out_specs=[
            pl.BlockSpec((1, NSA_WIDTH, tm), lambda i, j: (i, 0, j)),
            row(KV_W), row(KV_W), row(KV_W), row(KV_W), vt_spec, vt_spec, row(3 * CONV_WIDTH),
            pl.BlockSpec((1, N_KV, GATE_PAD, tm), lambda i, j: (i, 0, 0, j)),
        ],
        out_shape=[
            jax.ShapeDtypeStruct((b, NSA_WIDTH, s), BF16),
            jax.ShapeDtypeStruct((b, s, KV_W), BF16),
            jax.ShapeDtypeStruct((b, s, KV_W), BF16),
            jax.ShapeDtypeStruct((b, s, KV_W), F32),
            jax.ShapeDtypeStruct((b, s, KV_W), F32),
            vt_shape, vt_shape,
            jax.ShapeDtypeStruct((b, s, 3 * CONV_WIDTH), BF16),
            jax.ShapeDtypeStruct((b, N_KV, GATE_PAD, s), F32),
        ],
        compiler_params=_params("arbitrary", "arbitrary"),
        name="in_projection",
    )(x, mod, g, w, cos_t, sin_t)


def _compress_kernel(tk_ref, tv_ref, pek_ref, pev_ref, w1k_ref, w1v_ref, w2k_ref, w2v_ref,
                     kc_ref, vct_ref):
    n_rows = tk_ref.shape[1] // CMP_STRIDE
    half = CMP_LEN // 2
    acc = [[None, None], [None, None]]
    for l in range(half):
        for kv, (t_ref, pe_ref, w1_ref) in enumerate(((tk_ref, pek_ref, w1k_ref),
                                                      (tv_ref, pev_ref, w1v_ref))):
            tok = t_ref[0, pl.ds(l, n_rows, stride=CMP_STRIDE), :]
            for part in range(2):
                idx = part * half + l
                y = _dot((tok + pe_ref[idx:idx + 1, :]).astype(BF16), w1_ref[idx])
                acc[kv][part] = y if acc[kv][part] is None else acc[kv][part] + y
    outs = []
    for kv, w2_ref in enumerate((w2k_ref, w2v_ref)):
        pre = acc[kv][0] + pltpu.roll(acc[kv][1], n_rows - 1, 0)
        out = _dot(_silu(pre).astype(BF16), w2_ref[...])
        rows = lax.broadcasted_iota(jnp.int32, out.shape, 0)
        outs.append(jnp.where(rows < n_rows - 1, out, 0.0))
    kc_ref[0] = outs[0].astype(BF16)
    vct_ref[0] = outs[1].T.astype(BF16)


def _compress(kcf, vcf, pek, pev, w1k, w1v, w2k, w2v):
    b, s, _ = kcf.shape
    n_rows = s // CMP_STRIDE
    const = lambda a: pl.BlockSpec(a.shape, lambda i: (0,) * a.ndim)
    return pl.pallas_call(
        _compress_kernel,
        grid=(b,),
        in_specs=[pl.BlockSpec((1, s, KV_W), lambda i: (i, 0, 0)),
                  pl.BlockSpec((1, s, KV_W), lambda i: (i, 0, 0)),
                  const(pek), const(pev), const(w1k), const(w1v), const(w2k), const(w2v)],
        out_specs=[pl.BlockSpec((1, n_rows, KV_W), lambda i: (i, 0, 0)),
                   pl.BlockSpec((1, KV_W, n_rows), lambda i: (i, 0, 0))],
        out_shape=[jax.ShapeDtypeStruct((b, n_rows, KV_W), BF16),
                   jax.ShapeDtypeStruct((b, KV_W, n_rows), BF16)],
        compiler_params=_params("arbitrary"),
        name="compress",
    )(kcf, vcf, pek, pev, w1k, w1v, w2k, w2v)


def _attn_kernel(qt_ref, kc_ref, vct_ref, ks_ref, vst_ref, kw_ref, vwt_ref, gt_ref, ov_ref,
                 o_ref, selb_ref, *chain_refs, tq, tk, n_sel, parts):
    t0 = pl.program_id(1) * tq
    rows = GROUP * tq
    t_lane = t0 + lax.broadcasted_iota(jnp.int32, (1, tq), 1)
    n_chain = 2 * N_KV * parts
    acc_refs = chain_refs[:n_chain]
    sc_refs = chain_refs[n_chain:]
    rows_c = rows // parts

    def cid(branch, h, part):
        return (branch * N_KV + h) * parts + part

    def over_group(m):
        return jnp.concatenate([m] * GROUP, axis=1)

    def over_part(m):
        return jnp.concatenate([m] * (GROUP // parts), axis=1)

    def compressed_branch(h, qt):
        n_cmp = kc_ref.shape[1]
        s_c = _dot(kc_ref[0], qt)
        cmp_end = lax.broadcasted_iota(jnp.int32, (n_cmp, 1), 0) * CMP_STRIDE + (CMP_LEN - 1)
        s_c = jnp.where(cmp_end <= over_group(t_lane), s_c, -jnp.inf)
        m_c = jnp.max(s_c, axis=0, keepdims=True)
        m_c = jnp.where(m_c == -jnp.inf, 0.0, m_c)
        p_c = jnp.exp2(s_c - m_c)
        p_c = p_c / jnp.maximum(jnp.sum(p_c, axis=0, keepdims=True), 1e-30)
        o_c = _dot(vct_ref[0, h * HEAD_DIM:(h + 1) * HEAD_DIM, :], p_c.astype(BF16))
        p_sum = p_c[:, 0:tq]
        for g in range(1, GROUP):
            p_sum = p_sum + p_c[:, g * tq:(g + 1) * tq]
        p_hi = p_sum.astype(BF16)
        p_lo = (p_sum - p_hi.astype(F32)).astype(BF16)
        imp = _dot(ov_ref[...], p_hi) + _dot(ov_ref[...], p_lo)

        n_blk = imp.shape[0]
        j_blk = lax.broadcasted_iota(jnp.int32, (n_blk, tq), 0).astype(F32)
        blk_q = (t_lane // SEL_BLOCK).astype(F32)
        causal = j_blk <= blk_q
        forced = causal & ((j_blk == 0.0) | (j_blk == blk_q) | (j_blk == blk_q - 1.0))
        val = jnp.where(forced, jnp.inf, jnp.where(causal, imp, -jnp.inf))
        bias = jnp.full((n_blk, tq), NEG, F32)
        for _ in range(n_sel):
            top = jnp.max(val, axis=0, keepdims=True)
            cand = (val == top) & (val > -jnp.inf)
            first = jnp.min(jnp.where(cand, j_blk, float(n_blk)), axis=0, keepdims=True)
            pick = j_blk == first
            bias = jnp.where(pick, 0.0, bias)
            val = jnp.where(pick, -jnp.inf, val)
        selb_ref[h] = bias
        return o_c

    qts, o_cs = [], []
    zeros = jnp.zeros((HEAD_DIM, rows), BF16)
    for h in range(N_KV):
        qh = jnp.concatenate(
            [qt_ref[0, (h * GROUP + g) * HEAD_DIM:(h * GROUP + g + 1) * HEAD_DIM, :]
             for g in range(GROUP)], axis=1)
        qt = jnp.concatenate([qh, zeros] if h == 0 else [zeros, qh], axis=0)
        qts.append(qt)
        o_cs.append(compressed_branch(h, qt))
    for acc_ref in acc_refs:
        acc_ref[...] = jnp.zeros_like(acc_ref)

    def scores(c, k_tile, h, mask_bias):
        s_ref = sc_refs[c]
        part = c % parts
        qt = qts[h][:, part * rows_c:(part + 1) * rows_c]
        s = _dot(k_tile, qt) + over_part(mask_bias)
        s_ref[...] = s
        m_tile = jnp.max(s, axis=0, keepdims=True)
        return m_tile, jnp.exp2(s_ref[...] - m_tile).astype(BF16)

    def flash_step(m_tile, p, vt_tile, m, acc_ref):
        m_new = jnp.maximum(m, m_tile)
        alpha = jnp.exp2(m - m_new)
        beta = jnp.exp2(m_tile - m_new)
        acc_ref[...] = alpha * acc_ref[...] + beta * _dot(vt_tile, p)
        return m_new

    u_sub = lax.broadcasted_iota(jnp.int32, (tk, 1), 0)
    hi_tile = (t0 + tq - 1) // tk
    lo_tile = jnp.maximum(t0 - (WINDOW - 1), 0) // tk
    n_win = hi_tile - lo_tile + 1
    per_blk = tk // SEL_BLOCK

    def block_bias(h, kt):
        return jnp.concatenate(
            [jnp.broadcast_to(selb_ref[h, pl.ds(kt * per_blk + jj, 1), :], (SEL_BLOCK, tq))
             for jj in range(per_blk)], axis=0)

    def k_tile(k_ref, kt):
        return k_ref[0, pl.ds(pl.multiple_of(kt * tk, tk), tk), :]

    def run_chains(jobs, ms):
        out = list(ms)
        waiting = None
        for c, k_t, h, bias, vt in jobs:
            res = scores(c, k_t, h, bias)
            if waiting is not None:
                c0, res0, vt0 = waiting
                out[c0] = flash_step(*res0, vt0, ms[c0], acc_refs[c0])
            waiting = (c, res, vt)
        c0, res0, vt0 = waiting
        out[c0] = flash_step(*res0, vt0, ms[c0], acc_refs[c0])
        return tuple(out)

    def near_body(i, ms):
        kt = hi_tile - i
        dist = t_lane - (kt * tk + u_sub)
        causal = dist >= 0
        win_bias = jnp.where(causal & (dist < WINDOW), 0.0, NEG)
        k_s = k_tile(ks_ref, kt)
        k_w = k_tile(kw_ref, kt)
        jobs = []
        for h in range(N_KV):
            sel_bias = jnp.where(causal, block_bias(h, kt), NEG)
            jobs += [(cid(0, h, p), k_s, h, sel_bias, vst_ref[0, kt, h]) for p in range(parts)]
        for h in range(N_KV):
            jobs += [(cid(1, h, p), k_w, h, win_bias, vwt_ref[0, kt, h]) for p in range(parts)]
        return run_chains(jobs, ms)

    n_sel_chain = N_KV * parts

    def far_body(i, ms):
        kt = hi_tile - i
        k_s = k_tile(ks_ref, kt)
        jobs = []
        for h in range(N_KV):
            bias = block_bias(h, kt)
            jobs += [(cid(0, h, p), k_s, h, bias, vst_ref[0, kt, h]) for p in range(parts)]
        return run_chains(jobs, ms)

    m_init = jnp.full((1, rows_c), NEG, F32)
    ms = near_body(0, (m_init,) * n_chain)
    ms = lax.fori_loop(1, n_win, near_body, ms)
    lax.fori_loop(n_win, hi_tile + 1, far_body, ms[:n_sel_chain])

    heads = []
    for h in range(N_KV):
        a_s = jnp.concatenate([acc_refs[cid(0, h, p)][...] for p in range(parts)], axis=1)
        a_w = jnp.concatenate([acc_refs[cid(1, h, p)][...] for p in range(parts)], axis=1)
        o_s = a_s[:HEAD_DIM] / jnp.maximum(a_s[HEAD_DIM:HEAD_DIM + 1], 1e-30)
        o_w = a_w[:HEAD_DIM] / jnp.maximum(a_w[HEAD_DIM:HEAD_DIM + 1], 1e-30)
        gate = gt_ref[0, h]
        for g in range(GROUP):
            sl = slice(g * tq, (g + 1) * tq)
            heads.append(gate[g:g + 1] * o_cs[h][:, sl]
                         + gate[GROUP + g:GROUP + g + 1] * o_s[:, sl]
                         + gate[2 * GROUP + g:2 * GROUP + g + 1] * o_w[:, sl])
    for pair in range(N_HEADS // 2):
        both = jnp.concatenate(heads[2 * pair:2 * pair + 2], axis=0)
        o_ref[0, :, pair * LANES:(pair + 1) * LANES] = both.T.astype(BF16)


def _attention(qt, kc, vct, ks, vst, kw, vwt, gt, ov, tq, tk):
    b, _, s = qt.shape
    n_cmp = kc.shape[1]
    n_kt = s // tk
    n_blk = ov.shape[0]
    n_sel = min(N_SEL, s // SEL_BLOCK)
    parts = 2
    n_chain = 2 * N_KV * parts
    kern = functools.partial(_attn_kernel, tq=tq, tk=tk, n_sel=n_sel, parts=parts)
    k_spec = pl.BlockSpec((1, s, KV_W), lambda i, j: (i, 0, 0))
    vt_spec = pl.BlockSpec((1, n_kt, N_KV, V_ROWS, tk), lambda i, j: (i, 0, 0, 0, 0))
    acc = pltpu.VMEM((V_ROWS, GROUP * tq // parts), F32)
    score = pltpu.VMEM((tk, GROUP * tq // parts), F32)
    return pl.pallas_call(
        kern,
        grid=(b, s // tq),
        in_specs=[
            pl.BlockSpec((1, NSA_WIDTH, tq), lambda i, j: (i, 0, j)),
            pl.BlockSpec((1, n_cmp, KV_W), lambda i, j: (i, 0, 0)),
            pl.BlockSpec((1, KV_W, n_cmp), lambda i, j: (i, 0, 0)),
            k_spec, vt_spec, k_spec, vt_spec,
            pl.BlockSpec((1, N_KV, GATE_PAD, tq), lambda i, j: (i, 0, 0, j)),
            pl.BlockSpec((n_blk, n_cmp), lambda i, j: (0, 0)),
        ],
        out_specs=pl.BlockSpec((1, tq, NSA_WIDTH), lambda i, j: (i, j, 0)),
        out_shape=jax.ShapeDtypeStruct((b, s, NSA_WIDTH), BF16),
        scratch_shapes=[pltpu.VMEM((N_KV, n_blk, tq), F32)] + [acc] * n_chain + [score] * n_chain,
        compiler_params=_params("arbitrary", "arbitrary"),
        name="nsa_attention",
    )(qt, kc, vct, ks, vst, kw, vwt, gt, ov)


def _shifted_taps(cur, prev, taps):
    t = cur.shape[0]
    ext = jnp.concatenate([prev, cur], axis=0)
    return taps[0:1] * ext[6:6 + t] + taps[1:2] * ext[7:7 + t] + taps[2:3] * cur


def _group_norm(v, g):
    r = lax.rsqrt(jnp.mean(v * v, axis=-1, keepdims=True) + EPS)
    return (v * r) * g


def _outproj_kernel(attn_ref, cb_ref, cc_ref, ch_ref, ccp_ref, chp_ref, x_ref, mod_ref,
                    ga_ref, gc_ref, cw_ref, w_ref, o_ref):
    f32 = lambda ref: ref[0].astype(F32)
    z = f32(cc_ref) * f32(ch_ref)
    z_halo = f32(ccp_ref) * f32(chp_ref)
    z_prev = jnp.where(pl.program_id(1) > 0, z_halo[8:], 0.0)
    conv = f32(cb_ref) * _shifted_taps(z, z_prev, cw_ref[...])
    mix_a = _group_norm(f32(attn_ref), ga_ref[...]).astype(BF16)
    mix_c = _group_norm(conv, gc_ref[...]).astype(BF16)
    y = _dot(mix_a, w_ref[:NSA_WIDTH, :]) + _dot(mix_c, w_ref[NSA_WIDTH:, :])
    o_ref[0] = x_ref[0] + mod_ref[0, 2:3, :] * y


def _out_projection(attn, conv3, x, mod, ga, gc, cw, w, tm):
    b, s, d = x.shape
    halo = tm // HALO_ROWS
    col = lambda c: pl.BlockSpec((1, tm, CONV_WIDTH), lambda i, j: (i, j, c))
    prev = lambda c: pl.BlockSpec((1, HALO_ROWS, CONV_WIDTH),
                                  lambda i, j: (i, jnp.maximum(j * halo - 1, 0), c))
    const = lambda shape: pl.BlockSpec(shape, lambda i, j: (0,) * len(shape))
    return pl.pallas_call(
        _outproj_kernel,
        grid=(b, s // tm),
        in_specs=[
            pl.BlockSpec((1, tm, NSA_WIDTH), lambda i, j: (i, j, 0)),
            col(0), col(1), col(2), prev(1), prev(2),
            pl.BlockSpec((1, tm, d), lambda i, j: (i, j, 0)),
            pl.BlockSpec((1, 6, d), lambda i, j: (i, 0, 0)),
            const((1, NSA_WIDTH)), const((1, CONV_WIDTH)), const((CONV_K, CONV_WIDTH)),
            const((NSA_WIDTH + CONV_WIDTH, d)),
        ],
        out_specs=pl.BlockSpec((1, tm, d), lambda i, j: (i, j, 0)),
        out_shape=jax.ShapeDtypeStruct((b, s, d), F32),
        compiler_params=_params("arbitrary", "arbitrary"),
        name="out_projection",
    )(attn, conv3, conv3, conv3, conv3, conv3, x, mod, ga, gc, cw, w)


def _ffn_kernel(x_ref, mod_ref, g_ref, wu_ref, cw_ref, wd_ref, fg_ref, o_ref,
                act_ref, carry_ref, *, d_ff, tf, final_norm):
    x = x_ref[0]
    h = _rms_mod(x, g_ref[...], mod_ref[0, 4:5, :], mod_ref[0, 3:4, :]).astype(BF16)

    @pl.when(pl.program_id(1) == 0)
    def _():
        carry_ref[...] = jnp.zeros_like(carry_ref)

    def conv_cols(cols):
        u = _dot(h, wu_ref[:, cols])
        prev = carry_ref[:, cols]
        carry_ref[:, cols] = u[u.shape[0] - 8:, :]
        return _shifted_taps(u, prev, cw_ref[:, cols])

    for c in range(d_ff // tf):
        gate = conv_cols(slice(c * tf, (c + 1) * tf))
        value = conv_cols(slice(d_ff + c * tf, d_ff + (c + 1) * tf))
        act_ref[:, c * tf:(c + 1) * tf] = (_silu(gate) * value).astype(BF16)
    y = _dot(act_ref[...], wd_ref[...])
    out = x + mod_ref[0, 5:6, :] * y
    if final_norm:
        out = _group_norm(out, fg_ref[...])
    o_ref[0] = out


def _ffn(x, mod, g, wu, cw, wd, fg, tm, final_norm):
    b, s, d = x.shape
    d_ff = wd.shape[0]
    tf = 256
    kern = functools.partial(_ffn_kernel, d_ff=d_ff, tf=tf, final_norm=final_norm)
    const = lambda shape: pl.BlockSpec(shape, lambda i, j: (0,) * len(shape),
                                       pipeline_mode=pl.Buffered(1))
    return pl.pallas_call(
        kern,
        grid=(b, s // tm),
        in_specs=[
            pl.BlockSpec((1, tm, d), lambda i, j: (i, j, 0)),
            pl.BlockSpec((1, 6, d), lambda i, j: (i, 0, 0)),
            const((1, d)), const((d, 2 * d_ff)), const((CONV_K, 2 * d_ff)), const((d_ff, d)),
            const((1, d)),
        ],
        out_specs=pl.BlockSpec((1, tm, d), lambda i, j: (i, j, 0)),
        out_shape=jax.ShapeDtypeStruct((b, s, d), F32),
        scratch_shapes=[pltpu.VMEM((tm, d_ff), BF16), pltpu.VMEM((8, 2 * d_ff), F32)],
        compiler_params=_params("arbitrary", "arbitrary"),
        name="conv_ffn",
    )(x, mod, g, wu, cw, wd, fg)


def _reorder_w_in(w):
    d = w.shape[0]
    o_gate = NSA_WIDTH + 6 * KV_W
    o_conv = o_gate + N_HEADS * N_BRANCH
    gates = w[:, o_gate:o_conv].reshape(d, N_KV, GROUP, N_BRANCH)
    gates = gates.transpose(0, 1, 3, 2).reshape(d, N_KV, N_BRANCH * GROUP)
    gates = jnp.pad(gates, ((0, 0), (0, 0), (0, LANES - N_BRANCH * GROUP)))
    return jnp.concatenate(
        [w[:, :o_gate], w[:, o_conv:], gates.reshape(d, N_KV * LANES)], axis=1).astype(BF16)


def _two_head_blockdiag(w):
    z = jnp.zeros_like(w)
    return jnp.concatenate([jnp.concatenate([w, z], axis=-1),
                            jnp.concatenate([z, w], axis=-1)], axis=-2)


def _compress_weights(pe, w1, w2):
    pe2 = jnp.concatenate([pe, pe], axis=1)
    w1_2 = _two_head_blockdiag(w1.reshape(CMP_LEN, HEAD_DIM, CMP_HIDDEN)).astype(BF16)
    w2_2 = _two_head_blockdiag(w2).astype(BF16)
    return pe2, w1_2, w2_2


def _overlap_matrix(s):
    n_cmp = s // CMP_STRIDE
    n_blk = max(s // SEL_BLOCK, 8)
    cmp_start = jnp.arange(n_cmp) * CMP_STRIDE
    blk_start = jnp.arange(n_blk) * SEL_BLOCK
    ov = ((cmp_start[None, :] < blk_start[:, None] + SEL_BLOCK)
          & (cmp_start[None, :] + CMP_LEN > blk_start[:, None])
          & (jnp.arange(n_cmp)[None, :] < n_cmp - 1))
    return ov.astype(BF16)


def kernel(x, c, positions, w_mod, b_mod, norm1_g, w_in, cmp_pe_k, cmp_w1_k, cmp_w2_k,
           cmp_pe_v, cmp_w1_v, cmp_w2_v, conv_w, grp_g_attn, grp_g_conv, w_out, norm2_g,
           ffn_up, ffn_conv, ffn_down, final_g):
    b, s, d = x.shape
    depth = w_in.shape[0]
    tm = min(512, s)
    tq = min(512, s)

    mod_all = _modulation(c, w_mod, b_mod).reshape(depth, b, 6, d)
    cos_t, sin_t = _rope_tables(positions)
    ov = _overlap_matrix(s)

    for l in range(depth):
        mod = mod_all[l]
        qt, ks, kw, kcf, vcf, vst, vwt, conv3, gt = _in_projection(
            x, mod, norm1_g[l].reshape(1, d), _reorder_w_in(w_in[l]), cos_t, sin_t, tm)
        pek, w1k, w2k = _compress_weights(cmp_pe_k[l], cmp_w1_k[l], cmp_w2_k[l])
        pev, w1v, w2v = _compress_weights(cmp_pe_v[l], cmp_w1_v[l], cmp_w2_v[l])
        kc, vct = _compress(kcf, vcf, pek, pev, w1k, w1v, w2k, w2v)
        attn = _attention(qt, kc, vct, ks, vst, kw, vwt, gt, ov, tq, tm)
        x = _out_projection(attn, conv3, x, mod, grp_g_attn[l].reshape(1, -1),
                            grp_g_conv[l].reshape(1, -1), conv_w[l], w_out[l].astype(BF16), tm)
        x = _ffn(x, mod, norm2_g[l].reshape(1, d), ffn_up[l].astype(BF16), ffn_conv[l],
                 ffn_down[l].astype(BF16), final_g.reshape(1, d), tm, l == depth - 1)
    return x
```

```python
import functools

import jax
import jax.numpy as jnp
from jax import lax
from jax.experimental import pallas as pl
from jax.experimental.pallas import tpu as pltpu

F32 = jnp.float32
BF16 = jnp.bfloat16

HEAD_DIM = 64
N_HEADS = 8
N_KV = 2
GROUP = N_HEADS // N_KV
N_BRANCH = 3
NSA_WIDTH = N_HEADS * HEAD_DIM
CONV_WIDTH = 512
KV_W = N_KV * HEAD_DIM
ROT_DIM = HEAD_DIM // 4
ROPE_THETA = 500000.0
CMP_LEN = 32
CMP_STRIDE = 16
CMP_HIDDEN = 128
SEL_BLOCK = 64
N_SEL = 8
WINDOW = 512
CONV_K = 3
EPS = 1e-6

LANES = 128
QPART = LANES
GATE_PAD = 16
NEG = -1e30
LOG2_E = 1.4426950408889634
ONES_ROWS = 16
V_ROWS = HEAD_DIM + ONES_ROWS
HALO_ROWS = 16
VMEM_LIMIT = 56 * 1024 * 1024

COL_Q = 0
COL_KV = NSA_WIDTH
COL_CONV = COL_KV + 6 * KV_W
COL_GATE = COL_CONV + 3 * CONV_WIDTH
N_IN_PAD = COL_GATE + N_KV * LANES


def _dot(a, b):
    return jnp.dot(a, b, preferred_element_type=F32)


def _silu(v):
    return v * jax.nn.sigmoid(v)


def _params(*sem):
    return pltpu.CompilerParams(dimension_semantics=sem, vmem_limit_bytes=VMEM_LIMIT)


def _mod_kernel(c_ref, w_ref, b_ref, o_ref):
    c = c_ref[...]
    o_ref[...] = _dot(_silu(c).astype(BF16), w_ref[...].astype(BF16)) + b_ref[...]


def _modulation(c, w_mod, b_mod):
    depth, d, n = w_mod.shape
    b = c.shape[0]
    tn = d
    return pl.pallas_call(
        _mod_kernel,
        grid=(depth, n // tn),
        in_specs=[
            pl.BlockSpec((b, d), lambda l, j: (0, 0)),
            pl.BlockSpec((None, d, tn), lambda l, j: (l, 0, j)),
            pl.BlockSpec((None, 1, tn), lambda l, j: (l, 0, j)),
        ],
        out_specs=pl.BlockSpec((None, b, tn), lambda l, j: (l, 0, j)),
        out_shape=jax.ShapeDtypeStruct((depth, b, n), F32),
        compiler_params=_params("arbitrary", "arbitrary"),
        name="modulation",
    )(c, w_mod, b_mod.reshape(depth, 1, n))


def _rope_kernel(pos_ref, freq_ref, cos_ref, sin_ref):
    ang = pos_ref[0].astype(F32) * freq_ref[...]
    c = jnp.cos(ang)
    s = jnp.sin(ang)
    rest = (HEAD_DIM - ROT_DIM, ang.shape[1])
    cos_ref[0] = jnp.concatenate([c, c, jnp.ones(rest, F32)] * 2, axis=0).T
    sin_ref[0] = jnp.concatenate([-s, s, jnp.zeros(rest, F32)] * 2, axis=0).T


def _rope_tables(positions):
    b, s = positions.shape
    half = ROT_DIM // 2
    freqs = ROPE_THETA ** (-jnp.arange(half, dtype=F32) / half)
    spec = pl.BlockSpec((1, s, LANES), lambda i: (i, 0, 0))
    return pl.pallas_call(
        _rope_kernel,
        grid=(b,),
        in_specs=[
            pl.BlockSpec((1, 1, s), lambda i: (i, 0, 0)),
            pl.BlockSpec((half, 1), lambda i: (0, 0)),
        ],
        out_specs=[spec, spec],
        out_shape=[jax.ShapeDtypeStruct((b, s, LANES), F32)] * 2,
        compiler_params=_params("arbitrary"),
        name="rope_tables",
    )(positions.reshape(b, 1, s), freqs.reshape(half, 1))


def _rms_mod(x, g, scale, shift):
    r = lax.rsqrt(jnp.mean(x * x, axis=-1, keepdims=True) + EPS)
    return (x * r) * g * (1.0 + scale) + shift


def _inproj_kernel(x_ref, mod_ref, g_ref, w_ref, cos_ref, sin_ref,
                   qt_ref, ks_ref, kw_ref, kc_ref, vc_ref, vst_ref, vwt_ref, conv_ref, gt_ref):
    h = _rms_mod(x_ref[0], g_ref[...], mod_ref[0, 1:2, :], mod_ref[0, 0:1, :]).astype(BF16)
    tm = h.shape[0]
    cos_t = cos_ref[0]
    sin_t = sin_ref[0]
    first = (lax.broadcasted_iota(jnp.int32, (1, LANES), 1) % HEAD_DIM) < ROT_DIM // 2

    def rope(y):
        partner = jnp.where(first, pltpu.roll(y, LANES - ROT_DIM // 2, 1),
                            pltpu.roll(y, ROT_DIM // 2, 1))
        return y * cos_t + partner * sin_t

    scale = HEAD_DIM ** -0.5 * LOG2_E
    yq = _dot(h, w_ref[:, COL_Q:COL_KV])
    for j in range(NSA_WIDTH // LANES):
        sl = slice(j * LANES, (j + 1) * LANES)
        qt_ref[0, sl, :] = (rope(yq[:, sl]) * scale).T.astype(BF16)

    ykv = _dot(h, w_ref[:, COL_KV:COL_CONV])
    part = lambda j: ykv[:, j * LANES:(j + 1) * LANES]
    kc_ref[0] = rope(part(0))
    vc_ref[0] = part(1)
    ks_ref[0] = rope(part(2)).astype(BF16)
    kw_ref[0] = rope(part(4)).astype(BF16)
    ones = jnp.ones((ONES_ROWS, tm), BF16)
    for vt_ref, j in ((vst_ref, 3), (vwt_ref, 5)):
        vt = part(j).T
        for hd in range(N_KV):
            vt_ref[0, 0, hd, :HEAD_DIM, :] = vt[hd * HEAD_DIM:(hd + 1) * HEAD_DIM].astype(BF16)
            vt_ref[0, 0, hd, HEAD_DIM:, :] = ones

    conv_ref[0] = _dot(h, w_ref[:, COL_CONV:COL_GATE]).astype(BF16)
    yg = _dot(h, w_ref[:, COL_GATE:N_IN_PAD])
    for hd in range(N_KV):
        gt = yg[:, hd * LANES:(hd + 1) * LANES].T
        gt_ref[0, hd] = jax.nn.sigmoid(gt[:GATE_PAD])


def _in_projection(x, mod, g, w, cos_t, sin_t, tm):
    b, s, d = x.shape
    n_t = s // tm
    row = lambda n: pl.BlockSpec((1, tm, n), lambda i, j: (i, j, 0))
    vt_spec = pl.BlockSpec((1, 1, N_KV, V_ROWS, tm), lambda i, j: (i, j, 0, 0, 0))
    vt_shape = jax.ShapeDtypeStruct((b, n_t, N_KV, V_ROWS, tm), BF16)
    return pl.pallas_call(
        _inproj_kernel,
        grid=(b, n_t),
        in_specs=[
            row(d),
            pl.BlockSpec((1, 6, d), lambda i, j: (i, 0, 0)),
            pl.BlockSpec((1, d), lambda i, j: (0, 0)),
            pl.BlockSpec((d, N_IN_PAD), lambda i, j: (0, 0)),
            row(LANES), row(LANES),
        ],
        out_specs=[
            pl.BlockSpec((1, NSA_WIDTH, tm), lambda i, j: (i, 0, j)),
            row(KV_W), row(KV_W), row(KV_W), row(KV_W), vt_spec, vt_spec, row(3 * CONV_WIDTH),
            pl.BlockSpec((1, N_KV, GATE_PAD, tm), lambda i, j: (i, 0, 0, j)),
        ],
        out_shape=[
            jax.ShapeDtypeStruct((b, NSA_WIDTH, s), BF16),
            jax.ShapeDtypeStruct((b, s, KV_W), BF16),
            jax.ShapeDtypeStruct((b, s, KV_W), BF16),
            jax.ShapeDtypeStruct((b, s, KV_W), F32),
            jax.ShapeDtypeStruct((b, s, KV_W), F32),
            vt_shape, vt_shape,
            jax.ShapeDtypeStruct((b, s, 3 * CONV_WIDTH), BF16),
            jax.ShapeDtypeStruct((b, N_KV, GATE_PAD, s), F32),
        ],
        compiler_params=_params("arbitrary", "arbitrary"),
        name="in_projection",
    )(x, mod, g, w, cos_t, sin_t)


def _compress_kernel(tk_ref, tv_ref, pek_ref, pev_ref, w1k_ref, w1v_ref, w2k_ref, w2v_ref,
                     kc_ref, vct_ref):
    n_rows = tk_ref.shape[1] // CMP_STRIDE
    half = CMP_LEN // 2
    acc = [[None, None], [None, None]]
    for l in range(half):
        for kv, (t_ref, pe_ref, w1_ref) in enumerate(((tk_ref, pek_ref, w1k_ref),
                                                      (tv_ref, pev_ref, w1v_ref))):
            tok = t_ref[0, pl.ds(l, n_rows, stride=CMP_STRIDE), :]
            for part in range(2):
                idx = part * half + l
                y = _dot((tok + pe_ref[idx:idx + 1, :]).astype(BF16), w1_ref[idx])
                acc[kv][part] = y if acc[kv][part] is None else acc[kv][part] + y
    outs = []
    for kv, w2_ref in enumerate((w2k_ref, w2v_ref)):
        pre = acc[kv][0] + pltpu.roll(acc[kv][1], n_rows - 1, 0)
        out = _dot(_silu(pre).astype(BF16), w2_ref[...])
        rows = lax.broadcasted_iota(jnp.int32, out.shape, 0)
        outs.append(jnp.where(rows < n_rows - 1, out, 0.0))
    kc_ref[0] = outs[0].astype(BF16)
    vct_ref[0] = outs[1].T.astype(BF16)


def _compress(kcf, vcf, pek, pev, w1k, w1v, w2k, w2v):
    b, s, _ = kcf.shape
    n_rows = s // CMP_STRIDE
    const = lambda a: pl.BlockSpec(a.shape, lambda i: (0,) * a.ndim)
    return pl.pallas_call(
        _compress_kernel,
        grid=(b,),
        in_specs=[pl.BlockSpec((1, s, KV_W), lambda i: (i, 0, 0)),
                  pl.BlockSpec((1, s, KV_W), lambda i: (i, 0, 0)),
                  const(pek), const(pev), const(w1k), const(w1v), const(w2k), const(w2v)],
        out_specs=[pl.BlockSpec((1, n_rows, KV_W), lambda i: (i, 0, 0)),
                   pl.BlockSpec((1, KV_W, n_rows), lambda i: (i, 0, 0))],
        out_shape=[jax.ShapeDtypeStruct((b, n_rows, KV_W), BF16),
                   jax.ShapeDtypeStruct((b, KV_W, n_rows), BF16)],
        compiler_params=_params("arbitrary"),
        name="compress",
    )(kcf, vcf, pek, pev, w1k, w1v, w2k, w2v)


def _attn_kernel(qt_ref, kc_ref, vct_ref, ks_ref, vst_ref, kw_ref, vwt_ref, gt_ref, ov_ref,
                 o_ref, selb_ref, *chain_refs, tq, tk, n_sel):
    t0 = pl.program_id(1) * tq
    t_lane = t0 + lax.broadcasted_iota(jnp.int32, (1, tq), 1)
    parts = tq // QPART
    rows_c = GROUP * QPART
    rows = parts * rows_c
    n_chain = 2 * N_KV * parts
    acc_refs = chain_refs[:n_chain]
    sc_refs = chain_refs[n_chain:]

    def cid(branch, h, part):
        return (branch * N_KV + h) * parts + part

    def part_cols(m, a):
        return m[:, a * QPART:(a + 1) * QPART]

    def over_heads(m):
        return jnp.concatenate([m] * GROUP, axis=1)

    def to_lanes(m):
        return jnp.concatenate([over_heads(part_cols(m, a)) for a in range(parts)], axis=1)

    def compressed_branch(h, qt):
        n_cmp = kc_ref.shape[1]
        s_c = _dot(kc_ref[0], qt)
        cmp_end = lax.broadcasted_iota(jnp.int32, (n_cmp, 1), 0) * CMP_STRIDE + (CMP_LEN - 1)
        s_c = jnp.where(cmp_end <= to_lanes(t_lane), s_c, -jnp.inf)
        m_c = jnp.max(s_c, axis=0, keepdims=True)
        m_c = jnp.where(m_c == -jnp.inf, 0.0, m_c)
        p_c = jnp.exp2(s_c - m_c)
        p_c = p_c / jnp.maximum(jnp.sum(p_c, axis=0, keepdims=True), 1e-30)
        o_c = _dot(vct_ref[0, h * HEAD_DIM:(h + 1) * HEAD_DIM, :], p_c.astype(BF16))
        p_sum = []
        for a in range(parts):
            blocks = [p_c[:, a * rows_c + g * QPART:a * rows_c + (g + 1) * QPART]
                      for g in range(GROUP)]
            p_sum.append(sum(blocks[1:], blocks[0]))
        p_sum = jnp.concatenate(p_sum, axis=1)
        p_hi = p_sum.astype(BF16)
        p_lo = (p_sum - p_hi.astype(F32)).astype(BF16)
        imp = _dot(ov_ref[...], p_hi) + _dot(ov_ref[...], p_lo)

        n_blk = imp.shape[0]
        j_blk = lax.broadcasted_iota(jnp.int32, (n_blk, tq), 0).astype(F32)
        blk_q = (t_lane // SEL_BLOCK).astype(F32)
        causal = j_blk <= blk_q
        forced = causal & ((j_blk == 0.0) | (j_blk == blk_q) | (j_blk == blk_q - 1.0))
        val = jnp.where(forced, jnp.inf, jnp.where(causal, imp, -jnp.inf))
        bias = jnp.full((n_blk, tq), NEG, F32)
        for _ in range(n_sel):
            top = jnp.max(val, axis=0, keepdims=True)
            cand = (val == top) & (val > -jnp.inf)
            first = jnp.min(jnp.where(cand, j_blk, float(n_blk)), axis=0, keepdims=True)
            pick = j_blk == first
            bias = jnp.where(pick, 0.0, bias)
            val = jnp.where(pick, -jnp.inf, val)
        for a in range(parts):
            selb_ref[h, a] = part_cols(bias, a)
        return o_c

    qts, o_cs = [], []
    zeros = jnp.zeros((HEAD_DIM, rows), BF16)
    for h in range(N_KV):
        qh = jnp.concatenate(
            [qt_ref[0, (h * GROUP + g) * HEAD_DIM:(h * GROUP + g + 1) * HEAD_DIM,
                    a * QPART:(a + 1) * QPART]
             for a in range(parts) for g in range(GROUP)], axis=1)
        qt = jnp.concatenate([qh, zeros] if h == 0 else [zeros, qh], axis=0)
        qts.append(qt)
        o_cs.append(compressed_branch(h, qt))
    for acc_ref in acc_refs:
        acc_ref[...] = jnp.zeros_like(acc_ref)

    def scores(c, k_rows, h, a, bias):
        n = k_rows.shape[0]
        s_ref = sc_refs[c]
        s = _dot(k_rows, qts[h][:, a * rows_c:(a + 1) * rows_c]) + over_heads(bias)
        s_ref[0:n, :] = s
        m_tile = jnp.max(s, axis=0, keepdims=True)
        return m_tile, jnp.exp2(s_ref[0:n, :] - m_tile).astype(BF16)

    def flash_step(m_tile, p, vt_cols, m, acc_ref):
        m_new = jnp.maximum(m, m_tile)
        alpha = jnp.exp2(m - m_new)
        beta = jnp.exp2(m_tile - m_new)
        acc_ref[...] = alpha * acc_ref[...] + beta * _dot(vt_cols, p)
        return m_new

    hi_tile = (t0 + tq - 1) // tk
    per_blk = tk // SEL_BLOCK

    def block_bias(h, kt, a, r0, r1):
        return jnp.concatenate(
            [jnp.broadcast_to(selb_ref[h, a, pl.ds(kt * per_blk + jj, 1), :], (SEL_BLOCK, QPART))
             for jj in range(r0 // SEL_BLOCK, r1 // SEL_BLOCK)], axis=0)

    def key_dist(kt, a, r0, r1):
        u = kt * tk + r0 + lax.broadcasted_iota(jnp.int32, (r1 - r0, 1), 0)
        return part_cols(t_lane, a) - u

    def k_tile(k_ref, kt):
        return k_ref[0, pl.ds(pl.multiple_of(kt * tk, tk), tk), :]

    def run_chains(jobs, ms):
        out = list(ms)
        waiting = None
        for c, h, a, k_rows, bias, vt_cols in jobs:
            res = scores(c, k_rows, h, a, bias)
            if waiting is not None:
                c0, res0, vt0 = waiting
                out[c0] = flash_step(*res0, vt0, ms[c0], acc_refs[c0])
            waiting = (c, res, vt_cols)
        c0, res0, vt0 = waiting
        out[c0] = flash_step(*res0, vt0, ms[c0], acc_refs[c0])
        return tuple(out)

    def tile_jobs(kt, sel_rows, win_rows, diagonal):
        k_s = k_tile(ks_ref, kt)
        jobs = []
        for h in range(N_KV):
            for a in range(parts):
                r0, r1 = sel_rows(a)
                bias = block_bias(h, kt, a, r0, r1)
                if diagonal:
                    bias = jnp.where(key_dist(kt, a, r0, r1) >= 0, bias, NEG)
                jobs.append((cid(0, h, a), h, a, k_s[r0:r1], bias, vst_ref[0, kt, h, :, r0:r1]))
        if win_rows is not None:
            k_w = k_tile(kw_ref, kt)
            for h in range(N_KV):
                for a in range(parts):
                    r0, r1 = win_rows(a)
                    dist = key_dist(kt, a, r0, r1)
                    bias = jnp.where((dist >= 0) & (dist < WINDOW), 0.0, NEG)
                    jobs.append((cid(1, h, a), h, a, k_w[r0:r1], bias,
                                 vwt_ref[0, kt, h, :, r0:r1]))
        return jobs

    m_init = jnp.full((1, rows_c), NEG, F32)
    diag = lambda a: (0, (a + 1) * QPART)
    whole = lambda a: (0, tk)
    ms = run_chains(tile_jobs(hi_tile, diag, diag, True), (m_init,) * n_chain)

    def prev_body(_, ms):
        return run_chains(tile_jobs(hi_tile - 1, whole, lambda a: (a * QPART, tk), False), ms)

    ms = lax.fori_loop(0, jnp.minimum(hi_tile, 1), prev_body, ms)

    def far_body(i, ms):
        return run_chains(tile_jobs(hi_tile - i, whole, None, False), ms)

    lax.fori_loop(2, hi_tile + 1, far_body, ms[:N_KV * parts])

    heads = []
    for h in range(N_KV):
        a_s = jnp.concatenate([acc_refs[cid(0, h, a)][...] for a in range(parts)], axis=1)
        a_w = jnp.concatenate([acc_refs[cid(1, h, a)][...] for a in range(parts)], axis=1)
        o_s = a_s[:HEAD_DIM] / jnp.maximum(a_s[HEAD_DIM:HEAD_DIM + 1], 1e-30)
        o_w = a_w[:HEAD_DIM] / jnp.maximum(a_w[HEAD_DIM:HEAD_DIM + 1], 1e-30)
        gate = gt_ref[0, h]
        for g in range(GROUP):
            blocks = []
            for a in range(parts):
                sl = slice(a * rows_c + g * QPART, a * rows_c + (g + 1) * QPART)
                gate_a = part_cols(gate, a)
                blocks.append(gate_a[g:g + 1] * o_cs[h][:, sl]
                              + gate_a[GROUP + g:GROUP + g + 1] * o_s[:, sl]
                              + gate_a[2 * GROUP + g:2 * GROUP + g + 1] * o_w[:, sl])
            heads.append(jnp.concatenate(blocks, axis=1))
    for pair in range(N_HEADS // 2):
        both = jnp.concatenate(heads[2 * pair:2 * pair + 2], axis=0)
        o_ref[0, :, pair * LANES:(pair + 1) * LANES] = both.T.astype(BF16)


def _attention(qt, kc, vct, ks, vst, kw, vwt, gt, ov, tq, tk):
    b, _, s = qt.shape
    n_cmp = kc.shape[1]
    n_kt = s // tk
    n_blk = ov.shape[0]
    n_sel = min(N_SEL, s // SEL_BLOCK)
    assert tq == tk == WINDOW, "the kernel's tile walk assumes query tile = key tile = window"
    n_chain = 2 * N_KV * (tq // QPART)
    kern = functools.partial(_attn_kernel, tq=tq, tk=tk, n_sel=n_sel)
    k_spec = pl.BlockSpec((1, s, KV_W), lambda i, j: (i, 0, 0))
    vt_spec = pl.BlockSpec((1, n_kt, N_KV, V_ROWS, tk), lambda i, j: (i, 0, 0, 0, 0))
    acc = pltpu.VMEM((V_ROWS, GROUP * QPART), F32)
    score = pltpu.VMEM((tk, GROUP * QPART), F32)
    return pl.pallas_call(
        kern,
        grid=(b, s // tq),
        in_specs=[
            pl.BlockSpec((1, NSA_WIDTH, tq), lambda i, j: (i, 0, j)),
            pl.BlockSpec((1, n_cmp, KV_W), lambda i, j: (i, 0, 0)),
            pl.BlockSpec((1, KV_W, n_cmp), lambda i, j: (i, 0, 0)),
            k_spec, vt_spec, k_spec, vt_spec,
            pl.BlockSpec((1, N_KV, GATE_PAD, tq), lambda i, j: (i, 0, 0, j)),
            pl.BlockSpec((n_blk, n_cmp), lambda i, j: (0, 0)),
        ],
        out_specs=pl.BlockSpec((1, tq, NSA_WIDTH), lambda i, j: (i, j, 0)),
        out_shape=jax.ShapeDtypeStruct((b, s, NSA_WIDTH), BF16),
        scratch_shapes=[pltpu.VMEM((N_KV, tq // QPART, n_blk, QPART), F32)]
        + [acc] * n_chain + [score] * n_chain,
        compiler_params=_params("arbitrary", "arbitrary"),
        name="nsa_attention",
    )(qt, kc, vct, ks, vst, kw, vwt, gt, ov)


def _shifted_taps(cur, prev, taps):
    t = cur.shape[0]
    ext = jnp.concatenate([prev, cur], axis=0)
    return taps[0:1] * ext[6:6 + t] + taps[1:2] * ext[7:7 + t] + taps[2:3] * cur


def _group_norm(v, g):
    r = lax.rsqrt(jnp.mean(v * v, axis=-1, keepdims=True) + EPS)
    return (v * r) * g


def _outproj_kernel(attn_ref, cb_ref, cc_ref, ch_ref, ccp_ref, chp_ref, x_ref, mod_ref,
                    ga_ref, gc_ref, cw_ref, w_ref, o_ref):
    f32 = lambda ref: ref[0].astype(F32)
    z = f32(cc_ref) * f32(ch_ref)
    z_halo = f32(ccp_ref) * f32(chp_ref)
    z_prev = jnp.where(pl.program_id(1) > 0, z_halo[8:], 0.0)
    conv = f32(cb_ref) * _shifted_taps(z, z_prev, cw_ref[...])
    mix_a = _group_norm(f32(attn_ref), ga_ref[...]).astype(BF16)
    mix_c = _group_norm(conv, gc_ref[...]).astype(BF16)
    y = _dot(mix_a, w_ref[:NSA_WIDTH, :]) + _dot(mix_c, w_ref[NSA_WIDTH:, :])
    o_ref[0] = x_ref[0] + mod_ref[0, 2:3, :] * y


def _out_projection(attn, conv3, x, mod, ga, gc, cw, w, tm):
    b, s, d = x.shape
    halo = tm // HALO_ROWS
    col = lambda c: pl.BlockSpec((1, tm, CONV_WIDTH), lambda i, j: (i, j, c))
    prev = lambda c: pl.BlockSpec((1, HALO_ROWS, CONV_WIDTH),
                                  lambda i, j: (i, jnp.maximum(j * halo - 1, 0), c))
    const = lambda shape: pl.BlockSpec(shape, lambda i, j: (0,) * len(shape))
    return pl.pallas_call(
        _outproj_kernel,
        grid=(b, s // tm),
        in_specs=[
            pl.BlockSpec((1, tm, NSA_WIDTH), lambda i, j: (i, j, 0)),
            col(0), col(1), col(2), prev(1), prev(2),
            pl.BlockSpec((1, tm, d), lambda i, j: (i, j, 0)),
            pl.BlockSpec((1, 6, d), lambda i, j: (i, 0, 0)),
            const((1, NSA_WIDTH)), const((1, CONV_WIDTH)), const((CONV_K, CONV_WIDTH)),
            const((NSA_WIDTH + CONV_WIDTH, d)),
        ],
        out_specs=pl.BlockSpec((1, tm, d), lambda i, j: (i, j, 0)),
        out_shape=jax.ShapeDtypeStruct((b, s, d), F32),
        compiler_params=_params("arbitrary", "arbitrary"),
        name="out_projection",
    )(attn, conv3, conv3, conv3, conv3, conv3, x, mod, ga, gc, cw, w)


def _ffn_kernel(x_ref, mod_ref, g_ref, wu_ref, cw_ref, wd_ref, fg_ref, o_ref,
                act_ref, carry_ref, *, d_ff, tf, final_norm):
    x = x_ref[0]
    h = _rms_mod(x, g_ref[...], mod_ref[0, 4:5, :], mod_ref[0, 3:4, :]).astype(BF16)

    @pl.when(pl.program_id(1) == 0)
    def _():
        carry_ref[...] = jnp.zeros_like(carry_ref)

    def conv_cols(cols):
        u = _dot(h, wu_ref[:, cols])
        prev = carry_ref[:, cols]
        carry_ref[:, cols] = u[u.shape[0] - 8:, :]
        return _shifted_taps(u, prev, cw_ref[:, cols])

    for c in range(d_ff // tf):
        gate = conv_cols(slice(c * tf, (c + 1) * tf))
        value = conv_cols(slice(d_ff + c * tf, d_ff + (c + 1) * tf))
        act_ref[:, c * tf:(c + 1) * tf] = (_silu(gate) * value).astype(BF16)
    y = _dot(act_ref[...], wd_ref[...])
    out = x + mod_ref[0, 5:6, :] * y
    if final_norm:
        out = _group_norm(out, fg_ref[...])
    o_ref[0] = out


def _ffn(x, mod, g, wu, cw, wd, fg, tm, final_norm):
    b, s, d = x.shape
    d_ff = wd.shape[0]
    tf = 256
    kern = functools.partial(_ffn_kernel, d_ff=d_ff, tf=tf, final_norm=final_norm)
    const = lambda shape: pl.BlockSpec(shape, lambda i, j: (0,) * len(shape),
                                       pipeline_mode=pl.Buffered(1))
    return pl.pallas_call(
        kern,
        grid=(b, s // tm),
        in_specs=[
            pl.BlockSpec((1, tm, d), lambda i, j: (i, j, 0)),
            pl.BlockSpec((1, 6, d), lambda i, j: (i, 0, 0)),
            const((1, d)), const((d, 2 * d_ff)), const((CONV_K, 2 * d_ff)), const((d_ff, d)),
            const((1, d)),
        ],
        out_specs=pl.BlockSpec((1, tm, d), lambda i, j: (i, j, 0)),
        out_shape=jax.ShapeDtypeStruct((b, s, d), F32),
        scratch_shapes=[pltpu.VMEM((tm, d_ff), BF16), pltpu.VMEM((8, 2 * d_ff), F32)],
        compiler_params=_params("arbitrary", "arbitrary"),
        name="conv_ffn",
    )(x, mod, g, wu, cw, wd, fg)


def _reorder_w_in(w):
    d = w.shape[0]
    o_gate = NSA_WIDTH + 6 * KV_W
    o_conv = o_gate + N_HEADS * N_BRANCH
    gates = w[:, o_gate:o_conv].reshape(d, N_KV, GROUP, N_BRANCH)
    gates = gates.transpose(0, 1, 3, 2).reshape(d, N_KV, N_BRANCH * GROUP)
    gates = jnp.pad(gates, ((0, 0), (0, 0), (0, LANES - N_BRANCH * GROUP)))
    return jnp.concatenate(
        [w[:, :o_gate], w[:, o_conv:], gates.reshape(d, N_KV * LANES)], axis=1).astype(BF16)


def _two_head_blockdiag(w):
    z = jnp.zeros_like(w)
    return jnp.concatenate([jnp.concatenate([w, z], axis=-1),
                            jnp.concatenate([z, w], axis=-1)], axis=-2)


def _compress_weights(pe, w1, w2):
    pe2 = jnp.concatenate([pe, pe], axis=1)
    w1_2 = _two_head_blockdiag(w1.reshape(CMP_LEN, HEAD_DIM, CMP_HIDDEN)).astype(BF16)
    w2_2 = _two_head_blockdiag(w2).astype(BF16)
    return pe2, w1_2, w2_2


def _overlap_matrix(s):
    n_cmp = s // CMP_STRIDE
    n_blk = max(s // SEL_BLOCK, 8)
    cmp_start = jnp.arange(n_cmp) * CMP_STRIDE
    blk_start = jnp.arange(n_blk) * SEL_BLOCK
    ov = ((cmp_start[None, :] < blk_start[:, None] + SEL_BLOCK)
          & (cmp_start[None, :] + CMP_LEN > blk_start[:, None])
          & (jnp.arange(n_cmp)[None, :] < n_cmp - 1))
    return ov.astype(BF16)


def kernel(x, c, positions, w_mod, b_mod, norm1_g, w_in, cmp_pe_k, cmp_w1_k, cmp_w2_k,
           cmp_pe_v, cmp_w1_v, cmp_w2_v, conv_w, grp_g_attn, grp_g_conv, w_out, norm2_g,
           ffn_up, ffn_conv, ffn_down, final_g):
    b, s, d = x.shape
    depth = w_in.shape[0]
    tm = min(512, s)
    tq = min(512, s)

    mod_all = _modulation(c, w_mod, b_mod).reshape(depth, b, 6, d)
    cos_t, sin_t = _rope_tables(positions)
    ov = _overlap_matrix(s)

    for l in range(depth):
        mod = mod_all[l]
        qt, ks, kw, kcf, vcf, vst, vwt, conv3, gt = _in_projection(
            x, mod, norm1_g[l].reshape(1, d), _reorder_w_in(w_in[l]), cos_t, sin_t, tm)
        pek, w1k, w2k = _compress_weights(cmp_pe_k[l], cmp_w1_k[l], cmp_w2_k[l])
        pev, w1v, w2v = _compress_weights(cmp_pe_v[l], cmp_w1_v[l], cmp_w2_v[l])
        kc, vct = _compress(kcf, vcf, pek, pev, w1k, w1v, w2k, w2v)
        attn = _attention(qt, kc, vct, ks, vst, kw, vwt, gt, ov, tq, tm)
        x = _out_projection(attn, conv3, x, mod, grp_g_attn[l].reshape(1, -1),
                            grp_g_conv[l].reshape(1, -1), conv_w[l], w_out[l].astype(BF16), tm)
        x = _ffn(x, mod, norm2_g[l].reshape(1, d), ffn_up[l].astype(BF16), ffn_conv[l],
                 ffn_down[l].astype(BF16), final_g.reshape(1, d), tm, l == depth - 1)
    return x
```

```python
import functools

import jax
import jax.numpy as jnp
from jax import lax
from jax.experimental import pallas as pl
from jax.experimental.pallas import tpu as pltpu

F32 = jnp.float32
BF16 = jnp.bfloat16

HEAD_DIM = 64
N_HEADS = 8
N_KV = 2
GROUP = N_HEADS // N_KV
N_BRANCH = 3
NSA_WIDTH = N_HEADS * HEAD_DIM
CONV_WIDTH = 512
KV_W = N_KV * HEAD_DIM
ROT_DIM = HEAD_DIM // 4
ROPE_THETA = 500000.0
CMP_LEN = 32
CMP_STRIDE = 16
CMP_HIDDEN = 128
SEL_BLOCK = 64
N_SEL = 8
WINDOW = 512
CONV_K = 3
EPS = 1e-6

LANES = 128
QPART = LANES
GATE_PAD = 16
NEG = -1e30
LOG2_E = 1.4426950408889634
ONES_ROWS = 16
V_ROWS = HEAD_DIM + ONES_ROWS
HALO_ROWS = 16
VMEM_LIMIT = 56 * 1024 * 1024

COL_Q = 0
COL_KV = NSA_WIDTH
COL_CONV = COL_KV + 6 * KV_W
COL_GATE = COL_CONV + 3 * CONV_WIDTH
N_IN_PAD = COL_GATE + N_KV * LANES


def _dot(a, b):
    return jnp.dot(a, b, preferred_element_type=F32)


def _silu_of_half(half):
    return half + half * jnp.tanh(half)


def _silu(v):
    return _silu_of_half(0.5 * v)


def _params(*sem):
    return pltpu.CompilerParams(dimension_semantics=sem, vmem_limit_bytes=VMEM_LIMIT)


def _mod_kernel(c_ref, w_ref, b_ref, o_ref):
    c = c_ref[...]
    o_ref[...] = _dot(_silu(c).astype(BF16), w_ref[...].astype(BF16)) + b_ref[...]


def _modulation(c, w_mod, b_mod):
    depth, d, n = w_mod.shape
    b = c.shape[0]
    tn = d
    return pl.pallas_call(
        _mod_kernel,
        grid=(depth, n // tn),
        in_specs=[
            pl.BlockSpec((b, d), lambda l, j: (0, 0)),
            pl.BlockSpec((None, d, tn), lambda l, j: (l, 0, j)),
            pl.BlockSpec((None, 1, tn), lambda l, j: (l, 0, j)),
        ],
        out_specs=pl.BlockSpec((None, b, tn), lambda l, j: (l, 0, j)),
        out_shape=jax.ShapeDtypeStruct((depth, b, n), F32),
        compiler_params=_params("arbitrary", "arbitrary"),
        name="modulation",
    )(c, w_mod, b_mod.reshape(depth, 1, n))


def _rope_kernel(pos_ref, freq_ref, cos_ref, sin_ref):
    ang = pos_ref[0].astype(F32) * freq_ref[...]
    c = jnp.cos(ang)
    s = jnp.sin(ang)
    rest = (HEAD_DIM - ROT_DIM, ang.shape[1])
    cos_ref[0] = jnp.concatenate([c, c, jnp.ones(rest, F32)] * 2, axis=0).T
    sin_ref[0] = jnp.concatenate([-s, s, jnp.zeros(rest, F32)] * 2, axis=0).T


def _rope_tables(positions):
    b, s = positions.shape
    half = ROT_DIM // 2
    freqs = ROPE_THETA ** (-jnp.arange(half, dtype=F32) / half)
    spec = pl.BlockSpec((1, s, LANES), lambda i: (i, 0, 0))
    return pl.pallas_call(
        _rope_kernel,
        grid=(b,),
        in_specs=[
            pl.BlockSpec((1, 1, s), lambda i: (i, 0, 0)),
            pl.BlockSpec((half, 1), lambda i: (0, 0)),
        ],
        out_specs=[spec, spec],
        out_shape=[jax.ShapeDtypeStruct((b, s, LANES), F32)] * 2,
        compiler_params=_params("arbitrary"),
        name="rope_tables",
    )(positions.reshape(b, 1, s), freqs.reshape(half, 1))


def _rms_mod(x, g, scale, shift):
    r = lax.rsqrt(jnp.mean(x * x, axis=-1, keepdims=True) + EPS)
    return (x * r) * (g * (1.0 + scale)) + shift


def _inproj_kernel(x_ref, mod_ref, g_ref, w_ref, cos_ref, sin_ref,
                   qt_ref, ks_ref, kw_ref, kc_ref, vc_ref, vst_ref, vwt_ref, conv_ref, gt_ref):
    h = _rms_mod(x_ref[0], g_ref[...], mod_ref[0, 1:2, :], mod_ref[0, 0:1, :]).astype(BF16)
    tm = h.shape[0]
    cos_t = cos_ref[0]
    sin_t = sin_ref[0]
    first = (lax.broadcasted_iota(jnp.int32, (1, LANES), 1) % HEAD_DIM) < ROT_DIM // 2

    def rope(y):
        partner = jnp.where(first, pltpu.roll(y, LANES - ROT_DIM // 2, 1),
                            pltpu.roll(y, ROT_DIM // 2, 1))
        return y * cos_t + partner * sin_t

    scale = HEAD_DIM ** -0.5 * LOG2_E
    yq = _dot(h, w_ref[:, COL_Q:COL_KV])
    for j in range(NSA_WIDTH // LANES):
        sl = slice(j * LANES, (j + 1) * LANES)
        qt_ref[0, sl, :] = (rope(yq[:, sl]) * scale).T.astype(BF16)

    ykv = _dot(h, w_ref[:, COL_KV:COL_CONV])
    part = lambda j: ykv[:, j * LANES:(j + 1) * LANES]
    kc_ref[0] = rope(part(0))
    vc_ref[0] = part(1)
    ks_ref[0] = rope(part(2)).astype(BF16)
    kw_ref[0] = rope(part(4)).astype(BF16)
    ones = jnp.ones((ONES_ROWS, tm), BF16)
    for vt_ref, j in ((vst_ref, 3), (vwt_ref, 5)):
        vt = part(j).T
        for hd in range(N_KV):
            vt_ref[0, 0, hd, :HEAD_DIM, :] = vt[hd * HEAD_DIM:(hd + 1) * HEAD_DIM].astype(BF16)
            vt_ref[0, 0, hd, HEAD_DIM:, :] = ones

    conv_ref[0] = _dot(h, w_ref[:, COL_CONV:COL_GATE]).astype(BF16)
    yg = _dot(h, w_ref[:, COL_GATE:N_IN_PAD])
    for hd in range(N_KV):
        gt = yg[:, hd * LANES:(hd + 1) * LANES].T
        gt_ref[0, hd] = jax.nn.sigmoid(gt[:GATE_PAD])


def _in_projection(x, mod, g, w, cos_t, sin_t, tm):
    b, s, d = x.shape
    n_t = s // tm
    row = lambda n: pl.BlockSpec((1, tm, n), lambda i, j: (i, j, 0))
    vt_spec = pl.BlockSpec((1, 1, N_KV, V_ROWS, tm), lambda i, j: (i, j, 0, 0, 0))
    vt_shape = jax.ShapeDtypeStruct((b, n_t, N_KV, V_ROWS, tm), BF16)
    return pl.pallas_call(
        _inproj_kernel,
        grid=(b, n_t),
        in_specs=[
            row(d),
            pl.BlockSpec((1, 6, d), lambda i, j: (i, 0, 0)),
            pl.BlockSpec((1, d), lambda i, j: (0, 0)),
            pl.BlockSpec((d, N_IN_PAD), lambda i, j: (0, 0)),
            row(LANES), row(LANES),
        ],
        out_specs=[
            pl.BlockSpec((1, NSA_WIDTH, tm), lambda i, j: (i, 0, j)),
            row(KV_W), row(KV_W), row(KV_W), row(KV_W), vt_spec, vt_spec, row(3 * CONV_WIDTH),
            pl.BlockSpec((1, N_KV, GATE_PAD, tm), lambda i, j: (i, 0, 0, j)),
        ],
        out_shape=[
            jax.ShapeDtypeStruct((b, NSA_WIDTH, s), BF16),
            jax.ShapeDtypeStruct((b, s, KV_W), BF16),
            jax.ShapeDtypeStruct((b, s, KV_W), BF16),
            jax.ShapeDtypeStruct((b, s, KV_W), F32),
            jax.ShapeDtypeStruct((b, s, KV_W), F32),
            vt_shape, vt_shape,
            jax.ShapeDtypeStruct((b, s, 3 * CONV_WIDTH), BF16),
            jax.ShapeDtypeStruct((b, N_KV, GATE_PAD, s), F32),
        ],
        compiler_params=_params("arbitrary", "arbitrary"),
        name="in_projection",
    )(x, mod, g, w, cos_t, sin_t)


def _compress_kernel(tk_ref, tv_ref, pek_ref, pev_ref, w1k_ref, w1v_ref, w2k_ref, w2v_ref,
                     kc_ref, vct_ref):
    n_rows = tk_ref.shape[1] // CMP_STRIDE
    half = CMP_LEN // 2
    acc = [[None, None], [None, None]]
    for l in range(half):
        for kv, (t_ref, pe_ref, w1_ref) in enumerate(((tk_ref, pek_ref, w1k_ref),
                                                      (tv_ref, pev_ref, w1v_ref))):
            tok = t_ref[0, pl.ds(l, n_rows, stride=CMP_STRIDE), :]
            for part in range(2):
                idx = part * half + l
                y = _dot((tok + pe_ref[idx:idx + 1, :]).astype(BF16), w1_ref[idx])
                acc[kv][part] = y if acc[kv][part] is None else acc[kv][part] + y
    outs = []
    for kv, w2_ref in enumerate((w2k_ref, w2v_ref)):
        pre = acc[kv][0] + pltpu.roll(acc[kv][1], n_rows - 1, 0)
        out = _dot(_silu(pre).astype(BF16), w2_ref[...])
        rows = lax.broadcasted_iota(jnp.int32, out.shape, 0)
        outs.append(jnp.where(rows < n_rows - 1, out, 0.0))
    kc_ref[0] = outs[0].astype(BF16)
    vct_ref[0] = outs[1].T.astype(BF16)


def _compress(kcf, vcf, pek, pev, w1k, w1v, w2k, w2v):
    b, s, _ = kcf.shape
    n_rows = s // CMP_STRIDE
    const = lambda a: pl.BlockSpec(a.shape, lambda i: (0,) * a.ndim)
    return pl.pallas_call(
        _compress_kernel,
        grid=(b,),
        in_specs=[pl.BlockSpec((1, s, KV_W), lambda i: (i, 0, 0)),
                  pl.BlockSpec((1, s, KV_W), lambda i: (i, 0, 0)),
                  const(pek), const(pev), const(w1k), const(w1v), const(w2k), const(w2v)],
        out_specs=[pl.BlockSpec((1, n_rows, KV_W), lambda i: (i, 0, 0)),
                   pl.BlockSpec((1, KV_W, n_rows), lambda i: (i, 0, 0))],
        out_shape=[jax.ShapeDtypeStruct((b, n_rows, KV_W), BF16),
                   jax.ShapeDtypeStruct((b, KV_W, n_rows), BF16)],
        compiler_params=_params("arbitrary"),
        name="compress",
    )(kcf, vcf, pek, pev, w1k, w1v, w2k, w2v)


def _attn_kernel(qt_ref, kc_ref, vct_ref, ks_ref, vst_ref, kw_ref, vwt_ref, gt_ref, ov_ref,
                 o_ref, selb_ref, *chain_refs, tq, tk, n_sel):
    t0 = pl.program_id(1) * tq
    t_lane = t0 + lax.broadcasted_iota(jnp.int32, (1, tq), 1)
    parts = tq // QPART
    rows_c = GROUP * QPART
    rows = parts * rows_c
    n_chain = 2 * N_KV * parts
    acc_refs = chain_refs[:n_chain]
    sc_refs = chain_refs[n_chain:]

    def cid(branch, h, part):
        return (branch * N_KV + h) * parts + part

    def part_cols(m, a):
        return m[:, a * QPART:(a + 1) * QPART]

    def over_heads(m):
        return jnp.concatenate([m] * GROUP, axis=1)

    def to_lanes(m):
        return jnp.concatenate([over_heads(part_cols(m, a)) for a in range(parts)], axis=1)

    def compressed_branch(h, qt):
        n_cmp = kc_ref.shape[1]
        s_c = _dot(kc_ref[0], qt)
        cmp_end = lax.broadcasted_iota(jnp.int32, (n_cmp, 1), 0) * CMP_STRIDE + (CMP_LEN - 1)
        s_c = jnp.where(cmp_end <= to_lanes(t_lane), s_c, -jnp.inf)
        m_c = jnp.max(s_c, axis=0, keepdims=True)
        m_c = jnp.where(m_c == -jnp.inf, 0.0, m_c)
        p_c = jnp.exp2(s_c - m_c)
        p_c = p_c / jnp.maximum(jnp.sum(p_c, axis=0, keepdims=True), 1e-30)
        o_c = _dot(vct_ref[0, h * HEAD_DIM:(h + 1) * HEAD_DIM, :], p_c.astype(BF16))
        p_sum = []
        for a in range(parts):
            blocks = [p_c[:, a * rows_c + g * QPART:a * rows_c + (g + 1) * QPART]
                      for g in range(GROUP)]
            p_sum.append(sum(blocks[1:], blocks[0]))
        p_sum = jnp.concatenate(p_sum, axis=1)
        p_hi = p_sum.astype(BF16)
        p_lo = (p_sum - p_hi.astype(F32)).astype(BF16)
        imp = _dot(ov_ref[...], p_hi) + _dot(ov_ref[...], p_lo)

        n_blk = imp.shape[0]
        j_blk = lax.broadcasted_iota(jnp.int32, (n_blk, tq), 0).astype(F32)
        blk_q = (t_lane // SEL_BLOCK).astype(F32)
        causal = j_blk <= blk_q
        forced = causal & ((j_blk == 0.0) | (j_blk == blk_q) | (j_blk == blk_q - 1.0))
        val = jnp.where(forced, jnp.inf, jnp.where(causal, imp, -jnp.inf))
        bias = jnp.full((n_blk, tq), NEG, F32)
        for _ in range(n_sel):
            top = jnp.max(val, axis=0, keepdims=True)
            cand = (val == top) & (val > -jnp.inf)
            first = jnp.min(jnp.where(cand, j_blk, float(n_blk)), axis=0, keepdims=True)
            pick = j_blk == first
            bias = jnp.where(pick, 0.0, bias)
            val = jnp.where(pick, -jnp.inf, val)
        for a in range(parts):
            selb_ref[h, a] = part_cols(bias, a)
        return o_c

    qts, o_cs = [], []
    zeros = jnp.zeros((HEAD_DIM, rows), BF16)
    for h in range(N_KV):
        qh = jnp.concatenate(
            [qt_ref[0, (h * GROUP + g) * HEAD_DIM:(h * GROUP + g + 1) * HEAD_DIM,
                    a * QPART:(a + 1) * QPART]
             for a in range(parts) for g in range(GROUP)], axis=1)
        qt = jnp.concatenate([qh, zeros] if h == 0 else [zeros, qh], axis=0)
        qts.append(qt)
        o_cs.append(compressed_branch(h, qt))
    for acc_ref in acc_refs:
        acc_ref[...] = jnp.zeros_like(acc_ref)

    def scores(c, k_rows, h, a, bias):
        n = k_rows.shape[0]
        s_ref = sc_refs[c]
        s = _dot(k_rows, qts[h][:, a * rows_c:(a + 1) * rows_c]) + over_heads(bias)
        s_ref[0:n, :] = s
        m_tile = jnp.max(s, axis=0, keepdims=True)
        return m_tile, jnp.exp2(s_ref[0:n, :] - m_tile).astype(BF16)

    def flash_step(m_tile, p, vt_cols, m, acc_ref):
        m_new = jnp.maximum(m, m_tile)
        alpha = jnp.exp2(m - m_new)
        beta = jnp.exp2(m_tile - m_new)
        acc_ref[...] = alpha * acc_ref[...] + beta * _dot(vt_cols, p)
        return m_new

    hi_tile = (t0 + tq - 1) // tk
    per_blk = tk // SEL_BLOCK

    def block_bias(h, kt, a, r0, r1):
        return jnp.concatenate(
            [jnp.broadcast_to(selb_ref[h, a, pl.ds(kt * per_blk + jj, 1), :], (SEL_BLOCK, QPART))
             for jj in range(r0 // SEL_BLOCK, r1 // SEL_BLOCK)], axis=0)

    def key_dist(kt, a, r0, r1):
        u = kt * tk + r0 + lax.broadcasted_iota(jnp.int32, (r1 - r0, 1), 0)
        return part_cols(t_lane, a) - u

    def k_tile(k_ref, kt):
        return k_ref[0, pl.ds(pl.multiple_of(kt * tk, tk), tk), :]

    def run_chains(jobs, ms):
        out = list(ms)
        waiting = None
        for c, h, a, k_rows, bias, vt_cols in jobs:
            res = scores(c, k_rows, h, a, bias)
            if waiting is not None:
                c0, res0, vt0 = waiting
                out[c0] = flash_step(*res0, vt0, ms[c0], acc_refs[c0])
            waiting = (c, res, vt_cols)
        c0, res0, vt0 = waiting
        out[c0] = flash_step(*res0, vt0, ms[c0], acc_refs[c0])
        return tuple(out)

    def tile_jobs(kt, sel_rows, win_rows, diagonal):
        k_s = k_tile(ks_ref, kt)
        jobs = []
        for h in range(N_KV):
            for a in range(parts):
                r0, r1 = sel_rows(a)
                bias = block_bias(h, kt, a, r0, r1)
                if diagonal:
                    bias = jnp.where(key_dist(kt, a, r0, r1) >= 0, bias, NEG)
                jobs.append((cid(0, h, a), h, a, k_s[r0:r1], bias, vst_ref[0, kt, h, :, r0:r1]))
        if win_rows is not None:
            k_w = k_tile(kw_ref, kt)
            for h in range(N_KV):
                for a in range(parts):
                    r0, r1 = win_rows(a)
                    dist = key_dist(kt, a, r0, r1)
                    bias = jnp.where((dist >= 0) & (dist < WINDOW), 0.0, NEG)
                    jobs.append((cid(1, h, a), h, a, k_w[r0:r1], bias,
                                 vwt_ref[0, kt, h, :, r0:r1]))
        return jobs

    m_init = jnp.full((1, rows_c), NEG, F32)
    diag = lambda a: (0, (a + 1) * QPART)
    whole = lambda a: (0, tk)
    ms = run_chains(tile_jobs(hi_tile, diag, diag, True), (m_init,) * n_chain)

    def prev_body(_, ms):
        return run_chains(tile_jobs(hi_tile - 1, whole, lambda a: (a * QPART, tk), False), ms)

    ms = lax.fori_loop(0, jnp.minimum(hi_tile, 1), prev_body, ms)

    def far_body(i, ms):
        return run_chains(tile_jobs(hi_tile - i, whole, None, False), ms)

    lax.fori_loop(2, hi_tile + 1, far_body, ms[:N_KV * parts])

    heads = []
    for h in range(N_KV):
        a_s = jnp.concatenate([acc_refs[cid(0, h, a)][...] for a in range(parts)], axis=1)
        a_w = jnp.concatenate([acc_refs[cid(1, h, a)][...] for a in range(parts)], axis=1)
        o_s = a_s[:HEAD_DIM] / jnp.maximum(a_s[HEAD_DIM:HEAD_DIM + 1], 1e-30)
        o_w = a_w[:HEAD_DIM] / jnp.maximum(a_w[HEAD_DIM:HEAD_DIM + 1], 1e-30)
        gate = gt_ref[0, h]
        for g in range(GROUP):
            blocks = []
            for a in range(parts):
                sl = slice(a * rows_c + g * QPART, a * rows_c + (g + 1) * QPART)
                gate_a = part_cols(gate, a)
                blocks.append(gate_a[g:g + 1] * o_cs[h][:, sl]
                              + gate_a[GROUP + g:GROUP + g + 1] * o_s[:, sl]
                              + gate_a[2 * GROUP + g:2 * GROUP + g + 1] * o_w[:, sl])
            heads.append(jnp.concatenate(blocks, axis=1))
    for pair in range(N_HEADS // 2):
        both = jnp.concatenate(heads[2 * pair:2 * pair + 2], axis=0)
        o_ref[0, :, pair * LANES:(pair + 1) * LANES] = both.T.astype(BF16)


def _attention(qt, kc, vct, ks, vst, kw, vwt, gt, ov, tq, tk):
    b, _, s = qt.shape
    n_cmp = kc.shape[1]
    n_kt = s // tk
    n_blk = ov.shape[0]
    n_sel = min(N_SEL, s // SEL_BLOCK)
    assert tq == tk == WINDOW, "the kernel's tile walk assumes query tile = key tile = window"
    n_chain = 2 * N_KV * (tq // QPART)
    kern = functools.partial(_attn_kernel, tq=tq, tk=tk, n_sel=n_sel)
    k_spec = pl.BlockSpec((1, s, KV_W), lambda i, j: (i, 0, 0))
    vt_spec = pl.BlockSpec((1, n_kt, N_KV, V_ROWS, tk), lambda i, j: (i, 0, 0, 0, 0))
    acc = pltpu.VMEM((V_ROWS, GROUP * QPART), F32)
    score = pltpu.VMEM((tk, GROUP * QPART), F32)
    return pl.pallas_call(
        kern,
        grid=(b, s // tq),
        in_specs=[
            pl.BlockSpec((1, NSA_WIDTH, tq), lambda i, j: (i, 0, j)),
            pl.BlockSpec((1, n_cmp, KV_W), lambda i, j: (i, 0, 0)),
            pl.BlockSpec((1, KV_W, n_cmp), lambda i, j: (i, 0, 0)),
            k_spec, vt_spec, k_spec, vt_spec,
            pl.BlockSpec((1, N_KV, GATE_PAD, tq), lambda i, j: (i, 0, 0, j)),
            pl.BlockSpec((n_blk, n_cmp), lambda i, j: (0, 0)),
        ],
        out_specs=pl.BlockSpec((1, tq, NSA_WIDTH), lambda i, j: (i, j, 0)),
        out_shape=jax.ShapeDtypeStruct((b, s, NSA_WIDTH), BF16),
        scratch_shapes=[pltpu.VMEM((N_KV, tq // QPART, n_blk, QPART), F32)]
        + [acc] * n_chain + [score] * n_chain,
        compiler_params=_params("arbitrary", "arbitrary"),
        name="nsa_attention",
    )(qt, kc, vct, ks, vst, kw, vwt, gt, ov)


def _shifted_taps(cur, prev, taps):
    t = cur.shape[0]
    row = lax.broadcasted_iota(jnp.int32, (8, 1), 0)

    def shifted(k):
        body = pltpu.roll(cur, k, 0)
        head = jnp.where(row < k, pltpu.roll(prev, k, 0), body[:8])
        return jnp.concatenate([head, body[8:]], axis=0)

    return taps[0:1] * shifted(2) + taps[1:2] * shifted(1) + taps[2:3] * cur


def _group_norm(v, g):
    r = lax.rsqrt(jnp.mean(v * v, axis=-1, keepdims=True) + EPS)
    return (v * r) * g


def _outproj_kernel(attn_ref, cb_ref, cc_ref, ch_ref, ccp_ref, chp_ref, x_ref, mod_ref,
                    ga_ref, gc_ref, cw_ref, w_ref, o_ref):
    f32 = lambda ref: ref[0].astype(F32)
    z = f32(cc_ref) * f32(ch_ref)
    z_halo = f32(ccp_ref) * f32(chp_ref)
    z_prev = jnp.where(pl.program_id(1) > 0, z_halo[8:], 0.0)
    conv = f32(cb_ref) * _shifted_taps(z, z_prev, cw_ref[...])
    mix_a = _group_norm(f32(attn_ref), ga_ref[...]).astype(BF16)
    mix_c = _group_norm(conv, gc_ref[...]).astype(BF16)
    y = _dot(mix_a, w_ref[:NSA_WIDTH, :]) + _dot(mix_c, w_ref[NSA_WIDTH:, :])
    o_ref[0] = x_ref[0] + mod_ref[0, 2:3, :] * y


def _out_projection(attn, conv3, x, mod, ga, gc, cw, w, tm):
    b, s, d = x.shape
    halo = tm // HALO_ROWS
    col = lambda c: pl.BlockSpec((1, tm, CONV_WIDTH), lambda i, j: (i, j, c))
    prev = lambda c: pl.BlockSpec((1, HALO_ROWS, CONV_WIDTH),
                                  lambda i, j: (i, jnp.maximum(j * halo - 1, 0), c))
    const = lambda shape: pl.BlockSpec(shape, lambda i, j: (0,) * len(shape))
    return pl.pallas_call(
        _outproj_kernel,
        grid=(b, s // tm),
        in_specs=[
            pl.BlockSpec((1, tm, NSA_WIDTH), lambda i, j: (i, j, 0)),
            col(0), col(1), col(2), prev(1), prev(2),
            pl.BlockSpec((1, tm, d), lambda i, j: (i, j, 0)),
            pl.BlockSpec((1, 6, d), lambda i, j: (i, 0, 0)),
            const((1, NSA_WIDTH)), const((1, CONV_WIDTH)), const((CONV_K, CONV_WIDTH)),
            const((NSA_WIDTH + CONV_WIDTH, d)),
        ],
        out_specs=pl.BlockSpec((1, tm, d), lambda i, j: (i, j, 0)),
        out_shape=jax.ShapeDtypeStruct((b, s, d), F32),
        compiler_params=_params("arbitrary", "arbitrary"),
        name="out_projection",
    )(attn, conv3, conv3, conv3, conv3, conv3, x, mod, ga, gc, cw, w)


def _ffn_kernel(x_ref, mod_ref, g_ref, wu_ref, cw_ref, wd_ref, fg_ref, o_ref,
                act_ref, carry_ref, *, d_ff, tf, final_norm):
    x = x_ref[0]
    h = _rms_mod(x, g_ref[...], mod_ref[0, 4:5, :], mod_ref[0, 3:4, :]).astype(BF16)

    @pl.when(pl.program_id(1) == 0)
    def _():
        carry_ref[...] = jnp.zeros_like(carry_ref)

    def conv_cols(cols, tap_scale):
        u = _dot(h, wu_ref[:, cols])
        prev = carry_ref[:, cols]
        carry_ref[:, cols] = u[u.shape[0] - 8:, :]
        return _shifted_taps(u, prev, cw_ref[:, cols] * tap_scale)

    for c in range(d_ff // tf):
        half_gate = conv_cols(slice(c * tf, (c + 1) * tf), 0.5)
        value = conv_cols(slice(d_ff + c * tf, d_ff + (c + 1) * tf), 1.0)
        act_ref[:, c * tf:(c + 1) * tf] = (_silu_of_half(half_gate) * value).astype(BF16)
    y = _dot(act_ref[...], wd_ref[...])
    out = x + mod_ref[0, 5:6, :] * y
    if final_norm:
        out = _group_norm(out, fg_ref[...])
    o_ref[0] = out


def _ffn(x, mod, g, wu, cw, wd, fg, tm, final_norm):
    b, s, d = x.shape
    d_ff = wd.shape[0]
    tf = 256
    kern = functools.partial(_ffn_kernel, d_ff=d_ff, tf=tf, final_norm=final_norm)
    const = lambda shape: pl.BlockSpec(shape, lambda i, j: (0,) * len(shape),
                                       pipeline_mode=pl.Buffered(1))
    return pl.pallas_call(
        kern,
        grid=(b, s // tm),
        in_specs=[
            pl.BlockSpec((1, tm, d), lambda i, j: (i, j, 0)),
            pl.BlockSpec((1, 6, d), lambda i, j: (i, 0, 0)),
            const((1, d)), const((d, 2 * d_ff)), const((CONV_K, 2 * d_ff)), const((d_ff, d)),
            const((1, d)),
        ],
        out_specs=pl.BlockSpec((1, tm, d), lambda i, j: (i, j, 0)),
        out_shape=jax.ShapeDtypeStruct((b, s, d), F32),
        scratch_shapes=[pltpu.VMEM((tm, d_ff), BF16), pltpu.VMEM((8, 2 * d_ff), F32)],
        compiler_params=_params("arbitrary", "arbitrary"),
        name="conv_ffn",
    )(x, mod, g, wu, cw, wd, fg)


def _reorder_w_in(w):
    d = w.shape[0]
    o_gate = NSA_WIDTH + 6 * KV_W
    o_conv = o_gate + N_HEADS * N_BRANCH
    gates = w[:, o_gate:o_conv].reshape(d, N_KV, GROUP, N_BRANCH)
    gates = gates.transpose(0, 1, 3, 2).reshape(d, N_KV, N_BRANCH * GROUP)
    gates = jnp.pad(gates, ((0, 0), (0, 0), (0, LANES - N_BRANCH * GROUP)))
    return jnp.concatenate(
        [w[:, :o_gate], w[:, o_conv:], gates.reshape(d, N_KV * LANES)], axis=1).astype(BF16)


def _two_head_blockdiag(w):
    z = jnp.zeros_like(w)
    return jnp.concatenate([jnp.concatenate([w, z], axis=-1),
                            jnp.concatenate([z, w], axis=-1)], axis=-2)


def _compress_weights(pe, w1, w2):
    pe2 = jnp.concatenate([pe, pe], axis=1)
    w1_2 = _two_head_blockdiag(w1.reshape(CMP_LEN, HEAD_DIM, CMP_HIDDEN)).astype(BF16)
    w2_2 = _two_head_blockdiag(w2).astype(BF16)
    return pe2, w1_2, w2_2


def _overlap_matrix(s):
    n_cmp = s // CMP_STRIDE
    n_blk = max(s // SEL_BLOCK, 8)
    cmp_start = jnp.arange(n_cmp) * CMP_STRIDE
    blk_start = jnp.arange(n_blk) * SEL_BLOCK
    ov = ((cmp_start[None, :] < blk_start[:, None] + SEL_BLOCK)
          & (cmp_start[None, :] + CMP_LEN > blk_start[:, None])
          & (jnp.arange(n_cmp)[None, :] < n_cmp - 1))
    return ov.astype(BF16)


def kernel(x, c, positions, w_mod, b_mod, norm1_g, w_in, cmp_pe_k, cmp_w1_k, cmp_w2_k,
           cmp_pe_v, cmp_w1_v, cmp_w2_v, conv_w, grp_g_attn, grp_g_conv, w_out, norm2_g,
           ffn_up, ffn_conv, ffn_down, final_g):
    b, s, d = x.shape
    depth = w_in.shape[0]
    tm = min(512, s)
    tq = min(512, s)

    mod_all = _modulation(c, w_mod, b_mod).reshape(depth, b, 6, d)
    cos_t, sin_t = _rope_tables(positions)
    ov = _overlap_matrix(s)

    for l in range(depth):
        mod = mod_all[l]
        qt, ks, kw, kcf, vcf, vst, vwt, conv3, gt = _in_projection(
            x, mod, norm1_g[l].reshape(1, d), _reorder_w_in(w_in[l]), cos_t, sin_t, tm)
        pek, w1k, w2k = _compress_weights(cmp_pe_k[l], cmp_w1_k[l], cmp_w2_k[l])
        pev, w1v, w2v = _compress_weights(cmp_pe_v[l], cmp_w1_v[l], cmp_w2_v[l])
        kc, vct = _compress(kcf, vcf, pek, pev, w1k, w1v, w2k, w2v)
        attn = _attention(qt, kc, vct, ks, vst, kw, vwt, gt, ov, tq, tm)
        x = _out_projection(attn, conv3, x, mod, grp_g_attn[l].reshape(1, -1),
                            grp_g_conv[l].reshape(1, -1), conv_w[l], w_out[l].astype(BF16), tm)
        x = _ffn(x, mod, norm2_g[l].reshape(1, d), ffn_up[l].astype(BF16), ffn_conv[l],
                 ffn_down[l].astype(BF16), final_g.reshape(1, d), tm, l == depth - 1)
    return x
```

```python
import functools

import jax
import jax.numpy as jnp
from jax import lax
from jax.experimental import pallas as pl
from jax.experimental.pallas import tpu as pltpu

F32 = jnp.float32
BF16 = jnp.bfloat16

HEAD_DIM = 64
N_HEADS = 8
N_KV = 2
GROUP = N_HEADS // N_KV
N_BRANCH = 3
NSA_WIDTH = N_HEADS * HEAD_DIM
CONV_WIDTH = 512
KV_W = N_KV * HEAD_DIM
ROT_DIM = HEAD_DIM // 4
ROPE_THETA = 500000.0
CMP_LEN = 32
CMP_STRIDE = 16
CMP_HIDDEN = 128
SEL_BLOCK = 64
N_SEL = 8
WINDOW = 512
CONV_K = 3
EPS = 1e-6

LANES = 128
QPART = LANES
GATE_PAD = 16
NEG = -1e30
LOG2_E = 1.4426950408889634
ONES_ROWS = 16
V_ROWS = HEAD_DIM + ONES_ROWS
HALO_ROWS = 16
VMEM_LIMIT = 56 * 1024 * 1024

COL_Q = 0
COL_KV = NSA_WIDTH
COL_CONV = COL_KV + 6 * KV_W
COL_GATE = COL_CONV + 3 * CONV_WIDTH
N_IN_PAD = COL_GATE + N_KV * LANES


def _dot(a, b):
    return jnp.dot(a, b, preferred_element_type=F32)


def _silu_of_half(half):
    return half + half * jnp.tanh(half)


def _silu(v):
    return _silu_of_half(0.5 * v)


def _params(*sem):
    return pltpu.CompilerParams(dimension_semantics=sem, vmem_limit_bytes=VMEM_LIMIT)


def _mod_kernel(c_ref, w_ref, b_ref, o_ref):
    c = c_ref[...]
    o_ref[...] = _dot(_silu(c).astype(BF16), w_ref[...].astype(BF16)) + b_ref[...]


def _modulation(c, w_mod, b_mod):
    depth, d, n = w_mod.shape
    b = c.shape[0]
    tn = d
    return pl.pallas_call(
        _mod_kernel,
        grid=(depth, n // tn),
        in_specs=[
            pl.BlockSpec((b, d), lambda l, j: (0, 0)),
            pl.BlockSpec((None, d, tn), lambda l, j: (l, 0, j)),
            pl.BlockSpec((None, 1, tn), lambda l, j: (l, 0, j)),
        ],
        out_specs=pl.BlockSpec((None, b, tn), lambda l, j: (l, 0, j)),
        out_shape=jax.ShapeDtypeStruct((depth, b, n), F32),
        compiler_params=_params("arbitrary", "arbitrary"),
        name="modulation",
    )(c, w_mod, b_mod.reshape(depth, 1, n))


def _rope_kernel(pos_ref, freq_ref, cos_ref, sin_ref):
    ang = pos_ref[0].astype(F32) * freq_ref[...]
    c = jnp.cos(ang)
    s = jnp.sin(ang)
    rest = (HEAD_DIM - ROT_DIM, ang.shape[1])
    cos_ref[0] = jnp.concatenate([c, c, jnp.ones(rest, F32)] * 2, axis=0).T
    sin_ref[0] = jnp.concatenate([-s, s, jnp.zeros(rest, F32)] * 2, axis=0).T


def _rope_tables(positions):
    b, s = positions.shape
    half = ROT_DIM // 2
    freqs = ROPE_THETA ** (-jnp.arange(half, dtype=F32) / half)
    spec = pl.BlockSpec((1, s, LANES), lambda i: (i, 0, 0))
    return pl.pallas_call(
        _rope_kernel,
        grid=(b,),
        in_specs=[
            pl.BlockSpec((1, 1, s), lambda i: (i, 0, 0)),
            pl.BlockSpec((half, 1), lambda i: (0, 0)),
        ],
        out_specs=[spec, spec],
        out_shape=[jax.ShapeDtypeStruct((b, s, LANES), F32)] * 2,
        compiler_params=_params("arbitrary"),
        name="rope_tables",
    )(positions.reshape(b, 1, s), freqs.reshape(half, 1))


def _rms_mod(x, g, scale, shift):
    r = lax.rsqrt(jnp.mean(x * x, axis=-1, keepdims=True) + EPS)
    return (x * r) * (g * (1.0 + scale)) + shift


def _inproj_kernel(x_ref, mod_ref, g_ref, w_ref, cos_ref, sin_ref,
                   qt_ref, ks_ref, kw_ref, kc_ref, vc_ref, vst_ref, vwt_ref, conv_ref, gt_ref):
    h = _rms_mod(x_ref[0], g_ref[...], mod_ref[0, 1:2, :], mod_ref[0, 0:1, :]).astype(BF16)
    tm = h.shape[0]
    cos_t = cos_ref[0]
    sin_t = sin_ref[0]
    first = (lax.broadcasted_iota(jnp.int32, (1, LANES), 1) % HEAD_DIM) < ROT_DIM // 2

    def rope(y):
        partner = jnp.where(first, pltpu.roll(y, LANES - ROT_DIM // 2, 1),
                            pltpu.roll(y, ROT_DIM // 2, 1))
        return y * cos_t + partner * sin_t

    scale = HEAD_DIM ** -0.5 * LOG2_E
    yq = _dot(h, w_ref[:, COL_Q:COL_KV])
    for j in range(NSA_WIDTH // LANES):
        sl = slice(j * LANES, (j + 1) * LANES)
        qt_ref[0, sl, :] = (rope(yq[:, sl]) * scale).T.astype(BF16)

    ykv = _dot(h, w_ref[:, COL_KV:COL_CONV])
    part = lambda j: ykv[:, j * LANES:(j + 1) * LANES]
    kc_ref[0] = rope(part(0))
    vc_ref[0] = part(1)
    ks_ref[0] = rope(part(2)).astype(BF16)
    kw_ref[0] = rope(part(4)).astype(BF16)
    tk = vst_ref.shape[-1]
    ones = jnp.ones((ONES_ROWS, tk), BF16)
    for vt_ref, j in ((vst_ref, 3), (vwt_ref, 5)):
        vt = part(j).T
        for kk in range(tm // tk):
            for hd in range(N_KV):
                vt_ref[0, kk, hd, :HEAD_DIM, :] = vt[hd * HEAD_DIM:(hd + 1) * HEAD_DIM,
                                                     kk * tk:(kk + 1) * tk].astype(BF16)
                vt_ref[0, kk, hd, HEAD_DIM:, :] = ones

    conv_ref[0] = _dot(h, w_ref[:, COL_CONV:COL_GATE]).astype(BF16)
    yg = _dot(h, w_ref[:, COL_GATE:N_IN_PAD])
    for hd in range(N_KV):
        gt = yg[:, hd * LANES:(hd + 1) * LANES].T
        gt_ref[0, hd] = jax.nn.sigmoid(gt[:GATE_PAD])


def _in_projection(x, mod, g, w, cos_t, sin_t, tm, tk):
    b, s, d = x.shape
    n_t = s // tm
    row = lambda n: pl.BlockSpec((1, tm, n), lambda i, j: (i, j, 0))
    vt_spec = pl.BlockSpec((1, tm // tk, N_KV, V_ROWS, tk), lambda i, j: (i, j, 0, 0, 0))
    vt_shape = jax.ShapeDtypeStruct((b, s // tk, N_KV, V_ROWS, tk), BF16)
    return pl.pallas_call(
        _inproj_kernel,
        grid=(b, n_t),
        in_specs=[
            row(d),
            pl.BlockSpec((1, 6, d), lambda i, j: (i, 0, 0)),
            pl.BlockSpec((1, d), lambda i, j: (0, 0)),
            pl.BlockSpec((d, N_IN_PAD), lambda i, j: (0, 0), pipeline_mode=pl.Buffered(1)),
            row(LANES), row(LANES),
        ],
        out_specs=[
            pl.BlockSpec((1, NSA_WIDTH, tm), lambda i, j: (i, 0, j)),
            row(KV_W), row(KV_W), row(KV_W), row(KV_W), vt_spec, vt_spec, row(3 * CONV_WIDTH),
            pl.BlockSpec((1, N_KV, GATE_PAD, tm), lambda i, j: (i, 0, 0, j)),
        ],
        out_shape=[
            jax.ShapeDtypeStruct((b, NSA_WIDTH, s), BF16),
            jax.ShapeDtypeStruct((b, s, KV_W), BF16),
            jax.ShapeDtypeStruct((b, s, KV_W), BF16),
            jax.ShapeDtypeStruct((b, s, KV_W), F32),
            jax.ShapeDtypeStruct((b, s, KV_W), F32),
            vt_shape, vt_shape,
            jax.ShapeDtypeStruct((b, s, 3 * CONV_WIDTH), BF16),
            jax.ShapeDtypeStruct((b, N_KV, GATE_PAD, s), F32),
        ],
        compiler_params=_params("arbitrary", "arbitrary"),
        name="in_projection",
    )(x, mod, g, w, cos_t, sin_t)


def _compress_kernel(tk_ref, tv_ref, pek_ref, pev_ref, w1k_ref, w1v_ref, w2k_ref, w2v_ref,
                     kc_ref, vct_ref):
    n_rows = tk_ref.shape[1] // CMP_STRIDE
    half = CMP_LEN // 2
    acc = [[None, None], [None, None]]
    for l in range(half):
        for kv, (t_ref, pe_ref, w1_ref) in enumerate(((tk_ref, pek_ref, w1k_ref),
                                                      (tv_ref, pev_ref, w1v_ref))):
            tok = t_ref[0, pl.ds(l, n_rows, stride=CMP_STRIDE), :]
            for part in range(2):
                idx = part * half + l
                y = _dot((tok + pe_ref[idx:idx + 1, :]).astype(BF16), w1_ref[idx])
                acc[kv][part] = y if acc[kv][part] is None else acc[kv][part] + y
    outs = []
    for kv, w2_ref in enumerate((w2k_ref, w2v_ref)):
        pre = acc[kv][0] + pltpu.roll(acc[kv][1], n_rows - 1, 0)
        out = _dot(_silu(pre).astype(BF16), w2_ref[...])
        rows = lax.broadcasted_iota(jnp.int32, out.shape, 0)
        outs.append(jnp.where(rows < n_rows - 1, out, 0.0))
    kc_ref[0] = outs[0].astype(BF16)
    vct_ref[0] = outs[1].T.astype(BF16)


def _compress(kcf, vcf, pek, pev, w1k, w1v, w2k, w2v):
    b, s, _ = kcf.shape
    n_rows = s // CMP_STRIDE
    const = lambda a: pl.BlockSpec(a.shape, lambda i: (0,) * a.ndim)
    return pl.pallas_call(
        _compress_kernel,
        grid=(b,),
        in_specs=[pl.BlockSpec((1, s, KV_W), lambda i: (i, 0, 0)),
                  pl.BlockSpec((1, s, KV_W), lambda i: (i, 0, 0)),
                  const(pek), const(pev), const(w1k), const(w1v), const(w2k), const(w2v)],
        out_specs=[pl.BlockSpec((1, n_rows, KV_W), lambda i: (i, 0, 0)),
                   pl.BlockSpec((1, KV_W, n_rows), lambda i: (i, 0, 0))],
        out_shape=[jax.ShapeDtypeStruct((b, n_rows, KV_W), BF16),
                   jax.ShapeDtypeStruct((b, KV_W, n_rows), BF16)],
        compiler_params=_params("arbitrary"),
        name="compress",
    )(kcf, vcf, pek, pev, w1k, w1v, w2k, w2v)


def _attn_kernel(qt_ref, kc_ref, vct_ref, ks_ref, vst_ref, kw_ref, vwt_ref, gt_ref, ov_ref,
                 o_ref, selb_ref, *chain_refs, tq, tk, n_sel):
    t0 = pl.program_id(1) * tq
    t_lane = t0 + lax.broadcasted_iota(jnp.int32, (1, tq), 1)
    parts = tq // QPART
    rows_c = GROUP * QPART
    rows = parts * rows_c
    n_chain = 2 * N_KV * parts
    acc_refs = chain_refs[:n_chain]
    sc_refs = chain_refs[n_chain:]

    def cid(branch, h, part):
        return (branch * N_KV + h) * parts + part

    def part_cols(m, a):
        return m[:, a * QPART:(a + 1) * QPART]

    def over_heads(m):
        return jnp.concatenate([m] * GROUP, axis=1)

    def to_lanes(m):
        return jnp.concatenate([over_heads(part_cols(m, a)) for a in range(parts)], axis=1)

    def compressed_branch(h, qt):
        n_cmp = kc_ref.shape[1]
        s_c = _dot(kc_ref[0], qt)
        cmp_end = lax.broadcasted_iota(jnp.int32, (n_cmp, 1), 0) * CMP_STRIDE + (CMP_LEN - 1)
        s_c = jnp.where(cmp_end <= to_lanes(t_lane), s_c, -jnp.inf)
        m_c = jnp.max(s_c, axis=0, keepdims=True)
        m_c = jnp.where(m_c == -jnp.inf, 0.0, m_c)
        p_c = jnp.exp2(s_c - m_c)
        p_c = p_c / jnp.maximum(jnp.sum(p_c, axis=0, keepdims=True), 1e-30)
        o_c = _dot(vct_ref[0, h * HEAD_DIM:(h + 1) * HEAD_DIM, :], p_c.astype(BF16))
        p_sum = []
        for a in range(parts):
            blocks = [p_c[:, a * rows_c + g * QPART:a * rows_c + (g + 1) * QPART]
                      for g in range(GROUP)]
            p_sum.append(sum(blocks[1:], blocks[0]))
        p_sum = jnp.concatenate(p_sum, axis=1)
        p_hi = p_sum.astype(BF16)
        p_lo = (p_sum - p_hi.astype(F32)).astype(BF16)
        imp = _dot(ov_ref[...], p_hi) + _dot(ov_ref[...], p_lo)

        n_blk = imp.shape[0]
        j_blk = lax.broadcasted_iota(jnp.int32, (n_blk, tq), 0).astype(F32)
        blk_q = (t_lane // SEL_BLOCK).astype(F32)
        causal = j_blk <= blk_q
        forced = causal & ((j_blk == 0.0) | (j_blk == blk_q) | (j_blk == blk_q - 1.0))
        val = jnp.where(forced, jnp.inf, jnp.where(causal, imp, -jnp.inf))
        bias = jnp.full((n_blk, tq), NEG, F32)
        for _ in range(n_sel):
            top = jnp.max(val, axis=0, keepdims=True)
            cand = (val == top) & (val > -jnp.inf)
            first = jnp.min(jnp.where(cand, j_blk, float(n_blk)), axis=0, keepdims=True)
            pick = j_blk == first
            bias = jnp.where(pick, 0.0, bias)
            val = jnp.where(pick, -jnp.inf, val)
        for a in range(parts):
            selb_ref[h, a] = part_cols(bias, a)
        return o_c

    qts, o_cs = [], []
    zeros = jnp.zeros((HEAD_DIM, rows), BF16)
    for h in range(N_KV):
        qh = jnp.concatenate(
            [qt_ref[0, (h * GROUP + g) * HEAD_DIM:(h * GROUP + g + 1) * HEAD_DIM,
                    a * QPART:(a + 1) * QPART]
             for a in range(parts) for g in range(GROUP)], axis=1)
        qt = jnp.concatenate([qh, zeros] if h == 0 else [zeros, qh], axis=0)
        qts.append(qt)
        o_cs.append(compressed_branch(h, qt))
    for acc_ref in acc_refs:
        acc_ref[...] = jnp.zeros_like(acc_ref)

    def scores(c, k_rows, h, a, bias):
        n = k_rows.shape[0]
        s_ref = sc_refs[c]
        s = _dot(k_rows, qts[h][:, a * rows_c:(a + 1) * rows_c]) + over_heads(bias)
        s_ref[0:n, :] = s
        m_tile = jnp.max(s, axis=0, keepdims=True)
        return m_tile, jnp.exp2(s_ref[0:n, :] - m_tile).astype(BF16)

    def flash_step(m_tile, p, vt_cols, m, acc_ref):
        m_new = jnp.maximum(m, m_tile)
        alpha = jnp.exp2(m - m_new)
        beta = jnp.exp2(m_tile - m_new)
        acc_ref[...] = alpha * acc_ref[...] + beta * _dot(vt_cols, p)
        return m_new

    hi_tile = (t0 + tq - 1) // tk
    per_blk = tk // SEL_BLOCK

    def block_bias(h, kt, a, r0, r1):
        return jnp.concatenate(
            [jnp.broadcast_to(selb_ref[h, a, pl.ds(kt * per_blk + jj, 1), :], (SEL_BLOCK, QPART))
             for jj in range(r0 // SEL_BLOCK, r1 // SEL_BLOCK)], axis=0)

    def key_dist(kt, a, r0, r1):
        u = kt * tk + r0 + lax.broadcasted_iota(jnp.int32, (r1 - r0, 1), 0)
        return part_cols(t_lane, a) - u

    def k_tile(k_ref, kt):
        return k_ref[0, pl.ds(pl.multiple_of(kt * tk, tk), tk), :]

    def run_chains(jobs, ms):
        out = list(ms)
        waiting = None
        for c, h, a, k_rows, bias, vt_cols in jobs:
            res = scores(c, k_rows, h, a, bias)
            if waiting is not None:
                c0, res0, vt0 = waiting
                out[c0] = flash_step(*res0, vt0, ms[c0], acc_refs[c0])
            waiting = (c, res, vt_cols)
        c0, res0, vt0 = waiting
        out[c0] = flash_step(*res0, vt0, ms[c0], acc_refs[c0])
        return tuple(out)

    def tile_jobs(kt, sel_rows, win_rows, diagonal):
        k_s = k_tile(ks_ref, kt)
        jobs = []
        for h in range(N_KV):
            for a in range(parts):
                r0, r1 = sel_rows(a)
                bias = block_bias(h, kt, a, r0, r1)
                if diagonal:
                    bias = jnp.where(key_dist(kt, a, r0, r1) >= 0, bias, NEG)
                jobs.append((cid(0, h, a), h, a, k_s[r0:r1], bias, vst_ref[0, kt, h, :, r0:r1]))
        if win_rows is not None:
            k_w = k_tile(kw_ref, kt)
            for h in range(N_KV):
                for a in range(parts):
                    r0, r1 = win_rows(a)
                    dist = key_dist(kt, a, r0, r1)
                    bias = jnp.where((dist >= 0) & (dist < WINDOW), 0.0, NEG)
                    jobs.append((cid(1, h, a), h, a, k_w[r0:r1], bias,
                                 vwt_ref[0, kt, h, :, r0:r1]))
        return jobs

    m_init = jnp.full((1, rows_c), NEG, F32)
    diag = lambda a: (0, (a + 1) * QPART)
    whole = lambda a: (0, tk)
    ms = run_chains(tile_jobs(hi_tile, diag, diag, True), (m_init,) * n_chain)

    def prev_body(_, ms):
        return run_chains(tile_jobs(hi_tile - 1, whole, lambda a: (a * QPART, tk), False), ms)

    ms = lax.fori_loop(0, jnp.minimum(hi_tile, 1), prev_body, ms)

    def far_body(i, ms):
        return run_chains(tile_jobs(hi_tile - i, whole, None, False), ms)

    lax.fori_loop(2, hi_tile + 1, far_body, ms[:N_KV * parts])

    heads = []
    for h in range(N_KV):
        a_s = jnp.concatenate([acc_refs[cid(0, h, a)][...] for a in range(parts)], axis=1)
        a_w = jnp.concatenate([acc_refs[cid(1, h, a)][...] for a in range(parts)], axis=1)
        o_s = a_s[:HEAD_DIM] / jnp.maximum(a_s[HEAD_DIM:HEAD_DIM + 1], 1e-30)
        o_w = a_w[:HEAD_DIM] / jnp.maximum(a_w[HEAD_DIM:HEAD_DIM + 1], 1e-30)
        gate = gt_ref[0, h]
        for g in range(GROUP):
            blocks = []
            for a in range(parts):
                sl = slice(a * rows_c + g * QPART, a * rows_c + (g + 1) * QPART)
                gate_a = part_cols(gate, a)
                blocks.append(gate_a[g:g + 1] * o_cs[h][:, sl]
                              + gate_a[GROUP + g:GROUP + g + 1] * o_s[:, sl]
                              + gate_a[2 * GROUP + g:2 * GROUP + g + 1] * o_w[:, sl])
            heads.append(jnp.concatenate(blocks, axis=1))
    for pair in range(N_HEADS // 2):
        both = jnp.concatenate(heads[2 * pair:2 * pair + 2], axis=0)
        o_ref[0, :, pair * LANES:(pair + 1) * LANES] = both.T.astype(BF16)


def _attention(qt, kc, vct, ks, vst, kw, vwt, gt, ov, tq, tk):
    b, _, s = qt.shape
    n_cmp = kc.shape[1]
    n_kt = s // tk
    n_blk = ov.shape[0]
    n_sel = min(N_SEL, s // SEL_BLOCK)
    assert tq == tk == WINDOW, "the kernel's tile walk assumes query tile = key tile = window"
    n_chain = 2 * N_KV * (tq // QPART)
    kern = functools.partial(_attn_kernel, tq=tq, tk=tk, n_sel=n_sel)
    k_spec = pl.BlockSpec((1, s, KV_W), lambda i, j: (i, 0, 0))
    vt_spec = pl.BlockSpec((1, n_kt, N_KV, V_ROWS, tk), lambda i, j: (i, 0, 0, 0, 0))
    acc = pltpu.VMEM((V_ROWS, GROUP * QPART), F32)
    score = pltpu.VMEM((tk, GROUP * QPART), F32)
    return pl.pallas_call(
        kern,
        grid=(b, s // tq),
        in_specs=[
            pl.BlockSpec((1, NSA_WIDTH, tq), lambda i, j: (i, 0, j)),
            pl.BlockSpec((1, n_cmp, KV_W), lambda i, j: (i, 0, 0)),
            pl.BlockSpec((1, KV_W, n_cmp), lambda i, j: (i, 0, 0)),
            k_spec, vt_spec, k_spec, vt_spec,
            pl.BlockSpec((1, N_KV, GATE_PAD, tq), lambda i, j: (i, 0, 0, j)),
            pl.BlockSpec((n_blk, n_cmp), lambda i, j: (0, 0)),
        ],
        out_specs=pl.BlockSpec((1, tq, NSA_WIDTH), lambda i, j: (i, j, 0)),
        out_shape=jax.ShapeDtypeStruct((b, s, NSA_WIDTH), BF16),
        scratch_shapes=[pltpu.VMEM((N_KV, tq // QPART, n_blk, QPART), F32)]
        + [acc] * n_chain + [score] * n_chain,
        compiler_params=_params("arbitrary", "arbitrary"),
        name="nsa_attention",
    )(qt, kc, vct, ks, vst, kw, vwt, gt, ov)


def _shifted_taps(cur, prev, taps):
    t = cur.shape[0]
    row = lax.broadcasted_iota(jnp.int32, (8, 1), 0)

    def shifted(k):
        body = pltpu.roll(cur, k, 0)
        head = jnp.where(row < k, pltpu.roll(prev, k, 0), body[:8])
        return jnp.concatenate([head, body[8:]], axis=0)

    return taps[0:1] * shifted(2) + taps[1:2] * shifted(1) + taps[2:3] * cur


def _group_norm(v, g):
    r = lax.rsqrt(jnp.mean(v * v, axis=-1, keepdims=True) + EPS)
    return (v * r) * g


def _outproj_kernel(attn_ref, cb_ref, cc_ref, ch_ref, ccp_ref, chp_ref, x_ref, mod_ref,
                    ga_ref, gc_ref, cw_ref, w_ref, o_ref):
    f32 = lambda ref: ref[0].astype(F32)
    z = f32(cc_ref) * f32(ch_ref)
    z_halo = f32(ccp_ref) * f32(chp_ref)
    z_prev = jnp.where(pl.program_id(1) > 0, z_halo[8:], 0.0)
    conv = f32(cb_ref) * _shifted_taps(z, z_prev, cw_ref[...])
    mix_a = _group_norm(f32(attn_ref), ga_ref[...]).astype(BF16)
    mix_c = _group_norm(conv, gc_ref[...]).astype(BF16)
    y = _dot(mix_a, w_ref[:NSA_WIDTH, :]) + _dot(mix_c, w_ref[NSA_WIDTH:, :])
    o_ref[0] = x_ref[0] + mod_ref[0, 2:3, :] * y


def _out_projection(attn, conv3, x, mod, ga, gc, cw, w, tm):
    b, s, d = x.shape
    halo = tm // HALO_ROWS
    col = lambda c: pl.BlockSpec((1, tm, CONV_WIDTH), lambda i, j: (i, j, c))
    prev = lambda c: pl.BlockSpec((1, HALO_ROWS, CONV_WIDTH),
                                  lambda i, j: (i, jnp.maximum(j * halo - 1, 0), c))
    const = lambda shape: pl.BlockSpec(shape, lambda i, j: (0,) * len(shape))
    return pl.pallas_call(
        _outproj_kernel,
        grid=(b, s // tm),
        in_specs=[
            pl.BlockSpec((1, tm, NSA_WIDTH), lambda i, j: (i, j, 0)),
            col(0), col(1), col(2), prev(1), prev(2),
            pl.BlockSpec((1, tm, d), lambda i, j: (i, j, 0)),
            pl.BlockSpec((1, 6, d), lambda i, j: (i, 0, 0)),
            const((1, NSA_WIDTH)), const((1, CONV_WIDTH)), const((CONV_K, CONV_WIDTH)),
            const((NSA_WIDTH + CONV_WIDTH, d)),
        ],
        out_specs=pl.BlockSpec((1, tm, d), lambda i, j: (i, j, 0)),
        out_shape=jax.ShapeDtypeStruct((b, s, d), F32),
        compiler_params=_params("arbitrary", "arbitrary"),
        name="out_projection",
    )(attn, conv3, conv3, conv3, conv3, conv3, x, mod, ga, gc, cw, w)


def _ffn_kernel(x_ref, mod_ref, g_ref, wu_ref, cw_ref, wd_ref, fg_ref, o_ref,
                act_ref, carry_ref, *, d_ff, tf, final_norm):
    x = x_ref[0]
    h = _rms_mod(x, g_ref[...], mod_ref[0, 4:5, :], mod_ref[0, 3:4, :]).astype(BF16)

    @pl.when(pl.program_id(1) == 0)
    def _():
        carry_ref[...] = jnp.zeros_like(carry_ref)

    def conv_cols(cols, tap_scale):
        u = _dot(h, wu_ref[:, cols])
        prev = carry_ref[:, cols]
        carry_ref[:, cols] = u[u.shape[0] - 8:, :]
        return _shifted_taps(u, prev, cw_ref[:, cols] * tap_scale)

    for c in range(d_ff // tf):
        half_gate = conv_cols(slice(c * tf, (c + 1) * tf), 0.5)
        value = conv_cols(slice(d_ff + c * tf, d_ff + (c + 1) * tf), 1.0)
        act_ref[:, c * tf:(c + 1) * tf] = (_silu_of_half(half_gate) * value).astype(BF16)
    y = _dot(act_ref[...], wd_ref[...])
    out = x + mod_ref[0, 5:6, :] * y
    if final_norm:
        out = _group_norm(out, fg_ref[...])
    o_ref[0] = out


def _ffn(x, mod, g, wu, cw, wd, fg, tm, final_norm):
    b, s, d = x.shape
    d_ff = wd.shape[0]
    tf = 256
    kern = functools.partial(_ffn_kernel, d_ff=d_ff, tf=tf, final_norm=final_norm)
    const = lambda shape: pl.BlockSpec(shape, lambda i, j: (0,) * len(shape),
                                       pipeline_mode=pl.Buffered(1))
    return pl.pallas_call(
        kern,
        grid=(b, s // tm),
        in_specs=[
            pl.BlockSpec((1, tm, d), lambda i, j: (i, j, 0)),
            pl.BlockSpec((1, 6, d), lambda i, j: (i, 0, 0)),
            const((1, d)), const((d, 2 * d_ff)), const((CONV_K, 2 * d_ff)), const((d_ff, d)),
            const((1, d)),
        ],
        out_specs=pl.BlockSpec((1, tm, d), lambda i, j: (i, j, 0)),
        out_shape=jax.ShapeDtypeStruct((b, s, d), F32),
        scratch_shapes=[pltpu.VMEM((tm, d_ff), BF16), pltpu.VMEM((8, 2 * d_ff), F32)],
        compiler_params=_params("arbitrary", "arbitrary"),
        name="conv_ffn",
    )(x, mod, g, wu, cw, wd, fg)


def _reorder_w_in(w):
    d = w.shape[0]
    o_gate = NSA_WIDTH + 6 * KV_W
    o_conv = o_gate + N_HEADS * N_BRANCH
    gates = w[:, o_gate:o_conv].reshape(d, N_KV, GROUP, N_BRANCH)
    gates = gates.transpose(0, 1, 3, 2).reshape(d, N_KV, N_BRANCH * GROUP)
    gates = jnp.pad(gates, ((0, 0), (0, 0), (0, LANES - N_BRANCH * GROUP)))
    return jnp.concatenate(
        [w[:, :o_gate], w[:, o_conv:], gates.reshape(d, N_KV * LANES)], axis=1).astype(BF16)


def _two_head_blockdiag(w):
    z = jnp.zeros_like(w)
    return jnp.concatenate([jnp.concatenate([w, z], axis=-1),
                            jnp.concatenate([z, w], axis=-1)], axis=-2)


def _compress_weights(pe, w1, w2):
    pe2 = jnp.concatenate([pe, pe], axis=1)
    w1_2 = _two_head_blockdiag(w1.reshape(CMP_LEN, HEAD_DIM, CMP_HIDDEN)).astype(BF16)
    w2_2 = _two_head_blockdiag(w2).astype(BF16)
    return pe2, w1_2, w2_2


def _overlap_matrix(s):
    n_cmp = s // CMP_STRIDE
    n_blk = max(s // SEL_BLOCK, 8)
    cmp_start = jnp.arange(n_cmp) * CMP_STRIDE
    blk_start = jnp.arange(n_blk) * SEL_BLOCK
    ov = ((cmp_start[None, :] < blk_start[:, None] + SEL_BLOCK)
          & (cmp_start[None, :] + CMP_LEN > blk_start[:, None])
          & (jnp.arange(n_cmp)[None, :] < n_cmp - 1))
    return ov.astype(BF16)


def kernel(x, c, positions, w_mod, b_mod, norm1_g, w_in, cmp_pe_k, cmp_w1_k, cmp_w2_k,
           cmp_pe_v, cmp_w1_v, cmp_w2_v, conv_w, grp_g_attn, grp_g_conv, w_out, norm2_g,
           ffn_up, ffn_conv, ffn_down, final_g):
    b, s, d = x.shape
    depth = w_in.shape[0]
    tm = min(512, s)
    tq = min(512, s)
    tm_proj = min(1024, s)

    mod_all = _modulation(c, w_mod, b_mod).reshape(depth, b, 6, d)
    cos_t, sin_t = _rope_tables(positions)
    ov = _overlap_matrix(s)

    for l in range(depth):
        mod = mod_all[l]
        qt, ks, kw, kcf, vcf, vst, vwt, conv3, gt = _in_projection(
            x, mod, norm1_g[l].reshape(1, d), _reorder_w_in(w_in[l]), cos_t, sin_t, tm_proj, tm)
        pek, w1k, w2k = _compress_weights(cmp_pe_k[l], cmp_w1_k[l], cmp_w2_k[l])
        pev, w1v, w2v = _compress_weights(cmp_pe_v[l], cmp_w1_v[l], cmp_w2_v[l])
        kc, vct = _compress(kcf, vcf, pek, pev, w1k, w1v, w2k, w2v)
        attn = _attention(qt, kc, vct, ks, vst, kw, vwt, gt, ov, tq, tm)
        x = _out_projection(attn, conv3, x, mod, grp_g_attn[l].reshape(1, -1),
                            grp_g_conv[l].reshape(1, -1), conv_w[l], w_out[l].astype(BF16),
                            tm_proj)
        x = _ffn(x, mod, norm2_g[l].reshape(1, d), ffn_up[l].astype(BF16), ffn_conv[l],
                 ffn_down[l].astype(BF16), final_g.reshape(1, d), tm, l == depth - 1)
    return x
```

```python
import functools

import jax
import jax.numpy as jnp
from jax import lax
from jax.experimental import pallas as pl
from jax.experimental.pallas import tpu as pltpu

F32 = jnp.float32
BF16 = jnp.bfloat16

HEAD_DIM = 64
N_HEADS = 8
N_KV = 2
GROUP = N_HEADS // N_KV
N_BRANCH = 3
NSA_WIDTH = N_HEADS * HEAD_DIM
CONV_WIDTH = 512
KV_W = N_KV * HEAD_DIM
ROT_DIM = HEAD_DIM // 4
ROPE_THETA = 500000.0
CMP_LEN = 32
CMP_STRIDE = 16
CMP_HIDDEN = 128
SEL_BLOCK = 64
N_SEL = 8
WINDOW = 512
CONV_K = 3
EPS = 1e-6

LANES = 128
QPART = LANES
GATE_PAD = 16
NEG = -1e30
LOG2_E = 1.4426950408889634
ONES_ROWS = 16
V_ROWS = HEAD_DIM + ONES_ROWS
HALO_ROWS = 16
VMEM_LIMIT = 56 * 1024 * 1024

COL_Q = 0
COL_KV = NSA_WIDTH
COL_CONV = COL_KV + 6 * KV_W
COL_GATE = COL_CONV + 3 * CONV_WIDTH
N_IN_PAD = COL_GATE + N_KV * LANES


def _dot(a, b):
    return jnp.dot(a, b, preferred_element_type=F32)


def _silu_of_half(half):
    return half + half * jnp.tanh(half)


def _silu(v):
    return _silu_of_half(0.5 * v)


def _params(*sem):
    return pltpu.CompilerParams(dimension_semantics=sem, vmem_limit_bytes=VMEM_LIMIT)


def _mod_kernel(c_ref, w_ref, b_ref, o_ref):
    c = c_ref[...]
    o_ref[...] = _dot(_silu(c).astype(BF16), w_ref[...].astype(BF16)) + b_ref[...]


def _modulation(c, w_mod, b_mod):
    depth, d, n = w_mod.shape
    b = c.shape[0]
    tn = d
    return pl.pallas_call(
        _mod_kernel,
        grid=(depth, n // tn),
        in_specs=[
            pl.BlockSpec((b, d), lambda l, j: (0, 0)),
            pl.BlockSpec((None, d, tn), lambda l, j: (l, 0, j)),
            pl.BlockSpec((None, 1, tn), lambda l, j: (l, 0, j)),
        ],
        out_specs=pl.BlockSpec((None, b, tn), lambda l, j: (l, 0, j)),
        out_shape=jax.ShapeDtypeStruct((depth, b, n), F32),
        compiler_params=_params("arbitrary", "arbitrary"),
        name="modulation",
    )(c, w_mod, b_mod.reshape(depth, 1, n))


def _rope_kernel(pos_ref, freq_ref, cos_ref, sin_ref):
    ang = pos_ref[0].astype(F32) * freq_ref[...]
    c = jnp.cos(ang)
    s = jnp.sin(ang)
    rest = (HEAD_DIM - ROT_DIM, ang.shape[1])
    cos_ref[0] = jnp.concatenate([c, c, jnp.ones(rest, F32)] * 2, axis=0).T
    sin_ref[0] = jnp.concatenate([-s, s, jnp.zeros(rest, F32)] * 2, axis=0).T


def _rope_tables(positions):
    b, s = positions.shape
    half = ROT_DIM // 2
    freqs = ROPE_THETA ** (-jnp.arange(half, dtype=F32) / half)
    spec = pl.BlockSpec((1, s, LANES), lambda i: (i, 0, 0))
    return pl.pallas_call(
        _rope_kernel,
        grid=(b,),
        in_specs=[
            pl.BlockSpec((1, 1, s), lambda i: (i, 0, 0)),
            pl.BlockSpec((half, 1), lambda i: (0, 0)),
        ],
        out_specs=[spec, spec],
        out_shape=[jax.ShapeDtypeStruct((b, s, LANES), F32)] * 2,
        compiler_params=_params("arbitrary"),
        name="rope_tables",
    )(positions.reshape(b, 1, s), freqs.reshape(half, 1))


def _rms_mod(x, g, scale, shift):
    r = lax.rsqrt(jnp.mean(x * x, axis=-1, keepdims=True) + EPS)
    return (x * r) * (g * (1.0 + scale)) + shift


def _inproj_kernel(x_ref, mod_ref, g_ref, w_ref, cos_ref, sin_ref,
                   qt_ref, ks_ref, kw_ref, kc_ref, vc_ref, vst_ref, vwt_ref, conv_ref, gt_ref):
    h = _rms_mod(x_ref[0], g_ref[...], mod_ref[0, 1:2, :], mod_ref[0, 0:1, :]).astype(BF16)
    tm = h.shape[0]
    cos_t = cos_ref[0]
    sin_t = sin_ref[0]
    first = (lax.broadcasted_iota(jnp.int32, (1, LANES), 1) % HEAD_DIM) < ROT_DIM // 2

    def rope(y):
        partner = jnp.where(first, pltpu.roll(y, LANES - ROT_DIM // 2, 1),
                            pltpu.roll(y, ROT_DIM // 2, 1))
        return y * cos_t + partner * sin_t

    scale = HEAD_DIM ** -0.5 * LOG2_E
    yq = _dot(h, w_ref[:, COL_Q:COL_KV])
    for j in range(NSA_WIDTH // LANES):
        sl = slice(j * LANES, (j + 1) * LANES)
        qt_ref[0, sl, :] = (rope(yq[:, sl]) * scale).T.astype(BF16)

    ykv = _dot(h, w_ref[:, COL_KV:COL_CONV])
    part = lambda j: ykv[:, j * LANES:(j + 1) * LANES]
    kc_ref[0] = rope(part(0))
    vc_ref[0] = part(1)
    ks_ref[0] = rope(part(2)).astype(BF16)
    kw_ref[0] = rope(part(4)).astype(BF16)
    tk = vst_ref.shape[-1]
    ones = jnp.ones((ONES_ROWS, tk), BF16)
    for vt_ref, j in ((vst_ref, 3), (vwt_ref, 5)):
        vt = part(j).T
        for kk in range(tm // tk):
            for hd in range(N_KV):
                vt_ref[0, kk, hd, :HEAD_DIM, :] = vt[hd * HEAD_DIM:(hd + 1) * HEAD_DIM,
                                                     kk * tk:(kk + 1) * tk].astype(BF16)
                vt_ref[0, kk, hd, HEAD_DIM:, :] = ones

    conv_ref[0] = _dot(h, w_ref[:, COL_CONV:COL_GATE]).astype(BF16)
    yg = _dot(h, w_ref[:, COL_GATE:N_IN_PAD])
    for hd in range(N_KV):
        gt = yg[:, hd * LANES:(hd + 1) * LANES].T
        gt_ref[0, hd] = jax.nn.sigmoid(gt[:GATE_PAD])


def _in_projection(x, mod, g, w, cos_t, sin_t, tm, tk):
    b, s, d = x.shape
    n_t = s // tm
    row = lambda n: pl.BlockSpec((1, tm, n), lambda i, j: (i, j, 0))
    vt_spec = pl.BlockSpec((1, tm // tk, N_KV, V_ROWS, tk), lambda i, j: (i, j, 0, 0, 0))
    vt_shape = jax.ShapeDtypeStruct((b, s // tk, N_KV, V_ROWS, tk), BF16)
    return pl.pallas_call(
        _inproj_kernel,
        grid=(b, n_t),
        in_specs=[
            row(d),
            pl.BlockSpec((1, 6, d), lambda i, j: (i, 0, 0)),
            pl.BlockSpec((1, d), lambda i, j: (0, 0)),
            pl.BlockSpec((d, N_IN_PAD), lambda i, j: (0, 0), pipeline_mode=pl.Buffered(1)),
            row(LANES), row(LANES),
        ],
        out_specs=[
            pl.BlockSpec((1, NSA_WIDTH, tm), lambda i, j: (i, 0, j)),
            row(KV_W), row(KV_W), row(KV_W), row(KV_W), vt_spec, vt_spec, row(3 * CONV_WIDTH),
            pl.BlockSpec((1, N_KV, GATE_PAD, tm), lambda i, j: (i, 0, 0, j)),
        ],
        out_shape=[
            jax.ShapeDtypeStruct((b, NSA_WIDTH, s), BF16),
            jax.ShapeDtypeStruct((b, s, KV_W), BF16),
            jax.ShapeDtypeStruct((b, s, KV_W), BF16),
            jax.ShapeDtypeStruct((b, s, KV_W), F32),
            jax.ShapeDtypeStruct((b, s, KV_W), F32),
            vt_shape, vt_shape,
            jax.ShapeDtypeStruct((b, s, 3 * CONV_WIDTH), BF16),
            jax.ShapeDtypeStruct((b, N_KV, GATE_PAD, s), F32),
        ],
        compiler_params=_params("arbitrary", "arbitrary"),
        name="in_projection",
    )(x, mod, g, w, cos_t, sin_t)


def _compress_kernel(tk_ref, tv_ref, pek_ref, pev_ref, w1k_ref, w1v_ref, w2k_ref, w2v_ref,
                     kc_ref, vct_ref):
    n_rows = tk_ref.shape[1] // CMP_STRIDE
    half = CMP_LEN // 2
    acc = [[None, None], [None, None]]
    for l in range(half):
        for kv, (t_ref, pe_ref, w1_ref) in enumerate(((tk_ref, pek_ref, w1k_ref),
                                                      (tv_ref, pev_ref, w1v_ref))):
            tok = t_ref[0, pl.ds(l, n_rows, stride=CMP_STRIDE), :]
            for part in range(2):
                idx = part * half + l
                y = _dot((tok + pe_ref[idx:idx + 1, :]).astype(BF16), w1_ref[idx])
                acc[kv][part] = y if acc[kv][part] is None else acc[kv][part] + y
    outs = []
    for kv, w2_ref in enumerate((w2k_ref, w2v_ref)):
        pre = acc[kv][0] + pltpu.roll(acc[kv][1], n_rows - 1, 0)
        out = _dot(_silu(pre).astype(BF16), w2_ref[...])
        rows = lax.broadcasted_iota(jnp.int32, out.shape, 0)
        outs.append(jnp.where(rows < n_rows - 1, out, 0.0))
    kc_ref[0] = outs[0].astype(BF16)
    vct_ref[0] = outs[1].T.astype(BF16)


def _compress(kcf, vcf, pek, pev, w1k, w1v, w2k, w2v):
    b, s, _ = kcf.shape
    n_rows = s // CMP_STRIDE
    const = lambda a: pl.BlockSpec(a.shape, lambda i: (0,) * a.ndim)
    return pl.pallas_call(
        _compress_kernel,
        grid=(b,),
        in_specs=[pl.BlockSpec((1, s, KV_W), lambda i: (i, 0, 0)),
                  pl.BlockSpec((1, s, KV_W), lambda i: (i, 0, 0)),
                  const(pek), const(pev), const(w1k), const(w1v), const(w2k), const(w2v)],
        out_specs=[pl.BlockSpec((1, n_rows, KV_W), lambda i: (i, 0, 0)),
                   pl.BlockSpec((1, KV_W, n_rows), lambda i: (i, 0, 0))],
        out_shape=[jax.ShapeDtypeStruct((b, n_rows, KV_W), BF16),
                   jax.ShapeDtypeStruct((b, KV_W, n_rows), BF16)],
        compiler_params=_params("arbitrary"),
        name="compress",
    )(kcf, vcf, pek, pev, w1k, w1v, w2k, w2v)


def _attn_kernel(qt_ref, kc_ref, vct_ref, ks_ref, vst_ref, kw_ref, vwt_ref, gt_ref, ov_ref,
                 o_ref, selb_ref, *chain_refs, tq, tk, n_sel):
    t0 = pl.program_id(1) * tq
    t_lane = t0 + lax.broadcasted_iota(jnp.int32, (1, tq), 1)
    parts = tq // QPART
    rows_c = GROUP * QPART
    rows = parts * rows_c
    n_chain = 2 * N_KV * parts
    acc_refs = chain_refs[:n_chain]
    sc_refs = chain_refs[n_chain:]

    def cid(branch, h, part):
        return (branch * N_KV + h) * parts + part

    def part_cols(m, a):
        return m[:, a * QPART:(a + 1) * QPART]

    def over_heads(m):
        return jnp.concatenate([m] * GROUP, axis=1)

    def to_lanes(m):
        return jnp.concatenate([over_heads(part_cols(m, a)) for a in range(parts)], axis=1)

    def compressed_branch(h, qt):
        n_cmp = kc_ref.shape[1]
        s_c = _dot(kc_ref[0], qt)
        cmp_end = lax.broadcasted_iota(jnp.int32, (n_cmp, 1), 0) * CMP_STRIDE + (CMP_LEN - 1)
        s_c = jnp.where(cmp_end <= to_lanes(t_lane), s_c, -jnp.inf)
        m_c = jnp.max(s_c, axis=0, keepdims=True)
        m_c = jnp.where(m_c == -jnp.inf, 0.0, m_c)
        p_c = jnp.exp2(s_c - m_c)
        p_c = p_c * (1.0 / jnp.maximum(jnp.sum(p_c, axis=0, keepdims=True), 1e-30))
        o_c = _dot(vct_ref[0, h * HEAD_DIM:(h + 1) * HEAD_DIM, :], p_c.astype(BF16))
        p_sum = []
        for a in range(parts):
            blocks = [p_c[:, a * rows_c + g * QPART:a * rows_c + (g + 1) * QPART]
                      for g in range(GROUP)]
            p_sum.append(sum(blocks[1:], blocks[0]))
        p_sum = jnp.concatenate(p_sum, axis=1)
        p_hi = p_sum.astype(BF16)
        p_lo = (p_sum - p_hi.astype(F32)).astype(BF16)
        imp = _dot(ov_ref[...], p_hi) + _dot(ov_ref[...], p_lo)

        n_blk = imp.shape[0]
        j_blk = lax.broadcasted_iota(jnp.int32, (n_blk, tq), 0).astype(F32)
        blk_q = (t_lane // SEL_BLOCK).astype(F32)
        causal = j_blk <= blk_q
        forced = causal & ((j_blk == 0.0) | (j_blk == blk_q) | (j_blk == blk_q - 1.0))
        val = jnp.where(forced, jnp.inf, jnp.where(causal, imp, -jnp.inf))
        bias = jnp.full((n_blk, tq), NEG, F32)
        for _ in range(n_sel):
            top = jnp.max(val, axis=0, keepdims=True)
            cand = (val == top) & (val > -jnp.inf)
            first = jnp.min(jnp.where(cand, j_blk, float(n_blk)), axis=0, keepdims=True)
            pick = j_blk == first
            bias = jnp.where(pick, 0.0, bias)
            val = jnp.where(pick, -jnp.inf, val)
        for a in range(parts):
            selb_ref[h, a] = part_cols(bias, a)
        return o_c

    qts, o_cs = [], []
    zeros = jnp.zeros((HEAD_DIM, rows), BF16)
    for h in range(N_KV):
        qh = jnp.concatenate(
            [qt_ref[0, (h * GROUP + g) * HEAD_DIM:(h * GROUP + g + 1) * HEAD_DIM,
                    a * QPART:(a + 1) * QPART]
             for a in range(parts) for g in range(GROUP)], axis=1)
        qt = jnp.concatenate([qh, zeros] if h == 0 else [zeros, qh], axis=0)
        qts.append(qt)
        o_cs.append(compressed_branch(h, qt))

    def scores(c, k_rows, h, a, bias):
        n = k_rows.shape[0]
        s_ref = sc_refs[c]
        s = _dot(k_rows, qts[h][:, a * rows_c:(a + 1) * rows_c]) + over_heads(bias)
        s_ref[0:n, :] = s
        m_tile = jnp.max(s, axis=0, keepdims=True)
        return m_tile, jnp.exp2(s_ref[0:n, :] - m_tile).astype(BF16)

    def flash_step(m_tile, p, vt_cols, m, acc_ref):
        if m is None:
            acc_ref[...] = _dot(vt_cols, p)
            return m_tile
        m_new = jnp.maximum(m, m_tile)
        alpha = jnp.exp2(m - m_new)
        beta = jnp.exp2(m_tile - m_new)
        acc_ref[...] = alpha * acc_ref[...] + beta * _dot(vt_cols, p)
        return m_new

    hi_tile = (t0 + tq - 1) // tk
    per_blk = tk // SEL_BLOCK

    def block_bias(h, kt, a, r0, r1):
        return jnp.concatenate(
            [jnp.broadcast_to(selb_ref[h, a, pl.ds(kt * per_blk + jj, 1), :], (SEL_BLOCK, QPART))
             for jj in range(r0 // SEL_BLOCK, r1 // SEL_BLOCK)], axis=0)

    def key_dist(kt, a, r0, r1):
        u = kt * tk + r0 + lax.broadcasted_iota(jnp.int32, (r1 - r0, 1), 0)
        return part_cols(t_lane, a) - u

    def k_tile(k_ref, kt):
        return k_ref[0, pl.ds(pl.multiple_of(kt * tk, tk), tk), :]

    def run_chains(jobs, ms):
        out = list(ms)
        waiting = None
        for c, h, a, k_rows, bias, vt_cols in jobs:
            res = scores(c, k_rows, h, a, bias)
            if waiting is not None:
                c0, res0, vt0 = waiting
                out[c0] = flash_step(*res0, vt0, ms[c0], acc_refs[c0])
            waiting = (c, res, vt_cols)
        c0, res0, vt0 = waiting
        out[c0] = flash_step(*res0, vt0, ms[c0], acc_refs[c0])
        return tuple(out)

    def tile_jobs(kt, sel_rows, win_rows, diagonal):
        k_s = k_tile(ks_ref, kt)
        sel_jobs, win_jobs = [], []
        for h in range(N_KV):
            for a in range(parts):
                r0, r1 = sel_rows(a)
                bias = block_bias(h, kt, a, r0, r1)
                if diagonal:
                    bias = jnp.where(key_dist(kt, a, r0, r1) >= 0, bias, NEG)
                sel_jobs.append((cid(0, h, a), h, a, k_s[r0:r1], bias,
                                 vst_ref[0, kt, h, :, r0:r1]))
        if win_rows is not None:
            k_w = k_tile(kw_ref, kt)
            for h in range(N_KV):
                for a in range(parts):
                    r0, r1 = win_rows(a)
                    dist = key_dist(kt, a, r0, r1)
                    bias = jnp.where((dist >= 0) & (dist < WINDOW), 0.0, NEG)
                    win_jobs.append((cid(1, h, a), h, a, k_w[r0:r1], bias,
                                     vwt_ref[0, kt, h, :, r0:r1]))
        return win_jobs + sel_jobs if diagonal else sel_jobs + win_jobs

    diag = lambda a: (0, (a + 1) * QPART)
    whole = lambda a: (0, tk)
    ms = run_chains(tile_jobs(hi_tile, diag, diag, True), (None,) * n_chain)

    def prev_body(_, ms):
        return run_chains(tile_jobs(hi_tile - 1, whole, lambda a: (a * QPART, tk), False), ms)

    ms = lax.fori_loop(0, jnp.minimum(hi_tile, 1), prev_body, ms)

    def far_body(i, ms):
        return run_chains(tile_jobs(hi_tile - i, whole, None, False), ms)

    lax.fori_loop(2, hi_tile + 1, far_body, ms[:N_KV * parts])

    heads = []
    for h in range(N_KV):
        a_s = jnp.concatenate([acc_refs[cid(0, h, a)][...] for a in range(parts)], axis=1)
        a_w = jnp.concatenate([acc_refs[cid(1, h, a)][...] for a in range(parts)], axis=1)
        o_s = a_s[:HEAD_DIM] * (1.0 / jnp.maximum(a_s[HEAD_DIM:HEAD_DIM + 1], 1e-30))
        o_w = a_w[:HEAD_DIM] * (1.0 / jnp.maximum(a_w[HEAD_DIM:HEAD_DIM + 1], 1e-30))
        gate = gt_ref[0, h]
        for g in range(GROUP):
            blocks = []
            for a in range(parts):
                sl = slice(a * rows_c + g * QPART, a * rows_c + (g + 1) * QPART)
                gate_a = part_cols(gate, a)
                blocks.append(gate_a[g:g + 1] * o_cs[h][:, sl]
                              + gate_a[GROUP + g:GROUP + g + 1] * o_s[:, sl]
                              + gate_a[2 * GROUP + g:2 * GROUP + g + 1] * o_w[:, sl])
            heads.append(jnp.concatenate(blocks, axis=1))
    for pair in range(N_HEADS // 2):
        both = jnp.concatenate(heads[2 * pair:2 * pair + 2], axis=0)
        o_ref[0, :, pair * LANES:(pair + 1) * LANES] = both.T.astype(BF16)


def _attention(qt, kc, vct, ks, vst, kw, vwt, gt, ov, tq, tk):
    b, _, s = qt.shape
    n_cmp = kc.shape[1]
    n_kt = s // tk
    n_blk = ov.shape[0]
    n_sel = min(N_SEL, s // SEL_BLOCK)
    assert tq == tk == WINDOW, "the kernel's tile walk assumes query tile = key tile = window"
    n_chain = 2 * N_KV * (tq // QPART)
    kern = functools.partial(_attn_kernel, tq=tq, tk=tk, n_sel=n_sel)
    k_spec = pl.BlockSpec((1, s, KV_W), lambda i, j: (i, 0, 0))
    vt_spec = pl.BlockSpec((1, n_kt, N_KV, V_ROWS, tk), lambda i, j: (i, 0, 0, 0, 0))
    acc = pltpu.VMEM((V_ROWS, GROUP * QPART), F32)
    score = pltpu.VMEM((tk, GROUP * QPART), F32)
    return pl.pallas_call(
        kern,
        grid=(b, s // tq),
        in_specs=[
            pl.BlockSpec((1, NSA_WIDTH, tq), lambda i, j: (i, 0, j)),
            pl.BlockSpec((1, n_cmp, KV_W), lambda i, j: (i, 0, 0)),
            pl.BlockSpec((1, KV_W, n_cmp), lambda i, j: (i, 0, 0)),
            k_spec, vt_spec, k_spec, vt_spec,
            pl.BlockSpec((1, N_KV, GATE_PAD, tq), lambda i, j: (i, 0, 0, j)),
            pl.BlockSpec((n_blk, n_cmp), lambda i, j: (0, 0)),
        ],
        out_specs=pl.BlockSpec((1, tq, NSA_WIDTH), lambda i, j: (i, j, 0)),
        out_shape=jax.ShapeDtypeStruct((b, s, NSA_WIDTH), BF16),
        scratch_shapes=[pltpu.VMEM((N_KV, tq // QPART, n_blk, QPART), F32)]
        + [acc] * n_chain + [score] * n_chain,
        compiler_params=_params("arbitrary", "arbitrary"),
        name="nsa_attention",
    )(qt, kc, vct, ks, vst, kw, vwt, gt, ov)


def _shifted_taps(cur, prev, taps):
    t = cur.shape[0]
    row = lax.broadcasted_iota(jnp.int32, (8, 1), 0)

    def shifted(k):
        body = pltpu.roll(cur, k, 0)
        head = jnp.where(row < k, pltpu.roll(prev, k, 0), body[:8])
        return jnp.concatenate([head, body[8:]], axis=0)

    return taps[0:1] * shifted(2) + taps[1:2] * shifted(1) + taps[2:3] * cur


def _group_norm(v, g):
    r = lax.rsqrt(jnp.mean(v * v, axis=-1, keepdims=True) + EPS)
    return (v * r) * g


def _outproj_kernel(attn_ref, cb_ref, cc_ref, ch_ref, ccp_ref, chp_ref, x_ref, mod_ref,
                    ga_ref, gc_ref, cw_ref, w_ref, o_ref):
    f32 = lambda ref: ref[0].astype(F32)
    z = f32(cc_ref) * f32(ch_ref)
    z_halo = f32(ccp_ref) * f32(chp_ref)
    z_prev = jnp.where(pl.program_id(1) > 0, z_halo[8:], 0.0)
    conv = f32(cb_ref) * _shifted_taps(z, z_prev, cw_ref[...])
    mix_a = _group_norm(f32(attn_ref), ga_ref[...]).astype(BF16)
    mix_c = _group_norm(conv, gc_ref[...]).astype(BF16)
    y = _dot(mix_a, w_ref[:NSA_WIDTH, :]) + _dot(mix_c, w_ref[NSA_WIDTH:, :])
    o_ref[0] = x_ref[0] + mod_ref[0, 2:3, :] * y


def _out_projection(attn, conv3, x, mod, ga, gc, cw, w, tm):
    b, s, d = x.shape
    halo = tm // HALO_ROWS
    col = lambda c: pl.BlockSpec((1, tm, CONV_WIDTH), lambda i, j: (i, j, c))
    prev = lambda c: pl.BlockSpec((1, HALO_ROWS, CONV_WIDTH),
                                  lambda i, j: (i, jnp.maximum(j * halo - 1, 0), c))
    const = lambda shape: pl.BlockSpec(shape, lambda i, j: (0,) * len(shape))
    return pl.pallas_call(
        _outproj_kernel,
        grid=(b, s // tm),
        in_specs=[
            pl.BlockSpec((1, tm, NSA_WIDTH), lambda i, j: (i, j, 0)),
            col(0), col(1), col(2), prev(1), prev(2),
            pl.BlockSpec((1, tm, d), lambda i, j: (i, j, 0)),
            pl.BlockSpec((1, 6, d), lambda i, j: (i, 0, 0)),
            const((1, NSA_WIDTH)), const((1, CONV_WIDTH)), const((CONV_K, CONV_WIDTH)),
            const((NSA_WIDTH + CONV_WIDTH, d)),
        ],
        out_specs=pl.BlockSpec((1, tm, d), lambda i, j: (i, j, 0)),
        out_shape=jax.ShapeDtypeStruct((b, s, d), F32),
        compiler_params=_params("arbitrary", "arbitrary"),
        name="out_projection",
    )(attn, conv3, conv3, conv3, conv3, conv3, x, mod, ga, gc, cw, w)


def _ffn_kernel(x_ref, mod_ref, g_ref, wu_ref, cw_ref, wd_ref, fg_ref, o_ref,
                act_ref, carry_ref, *, d_ff, tf, final_norm):
    x = x_ref[0]
    h = _rms_mod(x, g_ref[...], mod_ref[0, 4:5, :], mod_ref[0, 3:4, :]).astype(BF16)

    @pl.when(pl.program_id(1) == 0)
    def _():
        carry_ref[...] = jnp.zeros_like(carry_ref)

    def conv_cols(cols, tap_scale):
        u = _dot(h, wu_ref[:, cols])
        prev = carry_ref[:, cols]
        carry_ref[:, cols] = u[u.shape[0] - 8:, :]
        return _shifted_taps(u, prev, cw_ref[:, cols] * tap_scale)

    for c in range(d_ff // tf):
        half_gate = conv_cols(slice(c * tf, (c + 1) * tf), 0.5)
        value = conv_cols(slice(d_ff + c * tf, d_ff + (c + 1) * tf), 1.0)
        act_ref[:, c * tf:(c + 1) * tf] = (_silu_of_half(half_gate) * value).astype(BF16)
    y = _dot(act_ref[...], wd_ref[...])
    out = x + mod_ref[0, 5:6, :] * y
    if final_norm:
        out = _group_norm(out, fg_ref[...])
    o_ref[0] = out


def _ffn(x, mod, g, wu, cw, wd, fg, tm, final_norm):
    b, s, d = x.shape
    d_ff = wd.shape[0]
    tf = 256
    kern = functools.partial(_ffn_kernel, d_ff=d_ff, tf=tf, final_norm=final_norm)
    const = lambda shape: pl.BlockSpec(shape, lambda i, j: (0,) * len(shape),
                                       pipeline_mode=pl.Buffered(1))
    return pl.pallas_call(
        kern,
        grid=(b, s // tm),
        in_specs=[
            pl.BlockSpec((1, tm, d), lambda i, j: (i, j, 0)),
            pl.BlockSpec((1, 6, d), lambda i, j: (i, 0, 0)),
            const((1, d)), const((d, 2 * d_ff)), const((CONV_K, 2 * d_ff)), const((d_ff, d)),
            const((1, d)),
        ],
        out_specs=pl.BlockSpec((1, tm, d), lambda i, j: (i, j, 0)),
        out_shape=jax.ShapeDtypeStruct((b, s, d), F32),
        scratch_shapes=[pltpu.VMEM((tm, d_ff), BF16), pltpu.VMEM((8, 2 * d_ff), F32)],
        compiler_params=_params("arbitrary", "arbitrary"),
        name="conv_ffn",
    )(x, mod, g, wu, cw, wd, fg)


def _reorder_w_in(w):
    d = w.shape[0]
    o_gate = NSA_WIDTH + 6 * KV_W
    o_conv = o_gate + N_HEADS * N_BRANCH
    gates = w[:, o_gate:o_conv].reshape(d, N_KV, GROUP, N_BRANCH)
    gates = gates.transpose(0, 1, 3, 2).reshape(d, N_KV, N_BRANCH * GROUP)
    gates = jnp.pad(gates, ((0, 0), (0, 0), (0, LANES - N_BRANCH * GROUP)))
    return jnp.concatenate(
        [w[:, :o_gate], w[:, o_conv:], gates.reshape(d, N_KV * LANES)], axis=1).astype(BF16)


def _two_head_blockdiag(w):
    z = jnp.zeros_like(w)
    return jnp.concatenate([jnp.concatenate([w, z], axis=-1),
                            jnp.concatenate([z, w], axis=-1)], axis=-2)


def _compress_weights(pe, w1, w2):
    pe2 = jnp.concatenate([pe, pe], axis=1)
    w1_2 = _two_head_blockdiag(w1.reshape(CMP_LEN, HEAD_DIM, CMP_HIDDEN)).astype(BF16)
    w2_2 = _two_head_blockdiag(w2).astype(BF16)
    return pe2, w1_2, w2_2


def _overlap_matrix(s):
    n_cmp = s // CMP_STRIDE
    n_blk = max(s // SEL_BLOCK, 8)
    cmp_start = jnp.arange(n_cmp) * CMP_STRIDE
    blk_start = jnp.arange(n_blk) * SEL_BLOCK
    ov = ((cmp_start[None, :] < blk_start[:, None] + SEL_BLOCK)
          & (cmp_start[None, :] + CMP_LEN > blk_start[:, None])
          & (jnp.arange(n_cmp)[None, :] < n_cmp - 1))
    return ov.astype(BF16)


def kernel(x, c, positions, w_mod, b_mod, norm1_g, w_in, cmp_pe_k, cmp_w1_k, cmp_w2_k,
           cmp_pe_v, cmp_w1_v, cmp_w2_v, conv_w, grp_g_attn, grp_g_conv, w_out, norm2_g,
           ffn_up, ffn_conv, ffn_down, final_g):
    b, s, d = x.shape
    depth = w_in.shape[0]
    tm = min(512, s)
    tq = min(512, s)
    tm_proj = min(1024, s)

    mod_all = _modulation(c, w_mod, b_mod).reshape(depth, b, 6, d)
    cos_t, sin_t = _rope_tables(positions)
    ov = _overlap_matrix(s)

    for l in range(depth):
        mod = mod_all[l]
        qt, ks, kw, kcf, vcf, vst, vwt, conv3, gt = _in_projection(
            x, mod, norm1_g[l].reshape(1, d), _reorder_w_in(w_in[l]), cos_t, sin_t, tm_proj, tm)
        pek, w1k, w2k = _compress_weights(cmp_pe_k[l], cmp_w1_k[l], cmp_w2_k[l])
        pev, w1v, w2v = _compress_weights(cmp_pe_v[l], cmp_w1_v[l], cmp_w2_v[l])
        kc, vct = _compress(kcf, vcf, pek, pev, w1k, w1v, w2k, w2v)
        attn = _attention(qt, kc, vct, ks, vst, kw, vwt, gt, ov, tq, tm)
        x = _out_projection(attn, conv3, x, mod, grp_g_attn[l].reshape(1, -1),
                            grp_g_conv[l].reshape(1, -1), conv_w[l], w_out[l].astype(BF16),
                            tm_proj)
        x = _ffn(x, mod, norm2_g[l].reshape(1, d), ffn_up[l].astype(BF16), ffn_conv[l],
                 ffn_down[l].astype(BF16), final_g.reshape(1, d), tm, l == depth - 1)
    return x
```

```python
import functools

import jax
import jax.numpy as jnp
from jax import lax
from jax.experimental import pallas as pl
from jax.experimental.pallas import tpu as pltpu

F32 = jnp.float32
BF16 = jnp.bfloat16

HEAD_DIM = 64
N_HEADS = 8
N_KV = 2
GROUP = N_HEADS // N_KV
N_BRANCH = 3
NSA_WIDTH = N_HEADS * HEAD_DIM
CONV_WIDTH = 512
KV_W = N_KV * HEAD_DIM
ROT_DIM = HEAD_DIM // 4
ROPE_THETA = 500000.0
CMP_LEN = 32
CMP_STRIDE = 16
CMP_HIDDEN = 128
SEL_BLOCK = 64
N_SEL = 8
WINDOW = 512
CONV_K = 3
EPS = 1e-6

LANES = 128
QPART = LANES
GATE_PAD = 16
NEG = -1e30
LOG2_E = 1.4426950408889634
ONES_ROWS = 16
V_ROWS = HEAD_DIM + ONES_ROWS
HALO_ROWS = 16
VMEM_LIMIT = 56 * 1024 * 1024

COL_Q = 0
COL_KV = NSA_WIDTH
COL_CONV = COL_KV + 6 * KV_W
COL_GATE = COL_CONV + 3 * CONV_WIDTH
N_IN_PAD = COL_GATE + N_KV * LANES


def _dot(a, b):
    return jnp.dot(a, b, preferred_element_type=F32)


def _silu_of_half(half):
    return half + half * jnp.tanh(half)


def _silu(v):
    return _silu_of_half(0.5 * v)


def _params(*sem):
    return pltpu.CompilerParams(dimension_semantics=sem, vmem_limit_bytes=VMEM_LIMIT)


def _mod_kernel(c_ref, w_ref, b_ref, o_ref):
    c = c_ref[...]
    o_ref[...] = _dot(_silu(c).astype(BF16), w_ref[...].astype(BF16)) + b_ref[...]


def _modulation(c, w_mod, b_mod):
    depth, d, n = w_mod.shape
    b = c.shape[0]
    tn = d
    return pl.pallas_call(
        _mod_kernel,
        grid=(depth, n // tn),
        in_specs=[
            pl.BlockSpec((b, d), lambda l, j: (0, 0)),
            pl.BlockSpec((None, d, tn), lambda l, j: (l, 0, j)),
            pl.BlockSpec((None, 1, tn), lambda l, j: (l, 0, j)),
        ],
        out_specs=pl.BlockSpec((None, b, tn), lambda l, j: (l, 0, j)),
        out_shape=jax.ShapeDtypeStruct((depth, b, n), F32),
        compiler_params=_params("arbitrary", "arbitrary"),
        name="modulation",
    )(c, w_mod, b_mod.reshape(depth, 1, n))


def _rope_kernel(pos_ref, freq_ref, cos_ref, sin_ref):
    ang = pos_ref[0].astype(F32) * freq_ref[...]
    c = jnp.cos(ang)
    s = jnp.sin(ang)
    rest = (HEAD_DIM - ROT_DIM, ang.shape[1])
    cos_ref[0] = jnp.concatenate([c, c, jnp.ones(rest, F32)] * 2, axis=0).T
    sin_ref[0] = jnp.concatenate([-s, s, jnp.zeros(rest, F32)] * 2, axis=0).T


def _rope_tables(positions):
    b, s = positions.shape
    half = ROT_DIM // 2
    freqs = ROPE_THETA ** (-jnp.arange(half, dtype=F32) / half)
    spec = pl.BlockSpec((1, s, LANES), lambda i: (i, 0, 0))
    return pl.pallas_call(
        _rope_kernel,
        grid=(b,),
        in_specs=[
            pl.BlockSpec((1, 1, s), lambda i: (i, 0, 0)),
            pl.BlockSpec((half, 1), lambda i: (0, 0)),
        ],
        out_specs=[spec, spec],
        out_shape=[jax.ShapeDtypeStruct((b, s, LANES), F32)] * 2,
        compiler_params=_params("arbitrary"),
        name="rope_tables",
    )(positions.reshape(b, 1, s), freqs.reshape(half, 1))


def _rms_mod(x, g, scale, shift):
    r = lax.rsqrt(jnp.mean(x * x, axis=-1, keepdims=True) + EPS)
    return (x * r) * (g * (1.0 + scale)) + shift


def _inproj_kernel(x_ref, mod_ref, g_ref, w_ref, cos_ref, sin_ref,
                   qt_ref, ks_ref, kw_ref, kc_ref, vc_ref, vst_ref, vwt_ref, conv_ref, gt_ref):
    h = _rms_mod(x_ref[0], g_ref[...], mod_ref[0, 1:2, :], mod_ref[0, 0:1, :]).astype(BF16)
    tm = h.shape[0]
    cos_t = cos_ref[0]
    sin_t = sin_ref[0]
    first = (lax.broadcasted_iota(jnp.int32, (1, LANES), 1) % HEAD_DIM) < ROT_DIM // 2

    def rope(y):
        partner = jnp.where(first, pltpu.roll(y, LANES - ROT_DIM // 2, 1),
                            pltpu.roll(y, ROT_DIM // 2, 1))
        return y * cos_t + partner * sin_t

    scale = HEAD_DIM ** -0.5 * LOG2_E
    yq = _dot(h, w_ref[:, COL_Q:COL_KV])
    for j in range(NSA_WIDTH // LANES):
        sl = slice(j * LANES, (j + 1) * LANES)
        qt_ref[0, sl, :] = (rope(yq[:, sl]) * scale).T.astype(BF16)

    ykv = _dot(h, w_ref[:, COL_KV:COL_CONV])
    part = lambda j: ykv[:, j * LANES:(j + 1) * LANES]
    kc_ref[0] = rope(part(0))
    vc_ref[0] = part(1)
    ks_ref[0] = rope(part(2)).astype(BF16)
    kw_ref[0] = rope(part(4)).astype(BF16)
    tk = vst_ref.shape[-1]
    ones = jnp.ones((ONES_ROWS, tk), BF16)
    for vt_ref, j in ((vst_ref, 3), (vwt_ref, 5)):
        vt = part(j).T
        for kk in range(tm // tk):
            for hd in range(N_KV):
                vt_ref[0, kk, hd, :HEAD_DIM, :] = vt[hd * HEAD_DIM:(hd + 1) * HEAD_DIM,
                                                     kk * tk:(kk + 1) * tk].astype(BF16)
                vt_ref[0, kk, hd, HEAD_DIM:, :] = ones

    conv_ref[0] = _dot(h, w_ref[:, COL_CONV:COL_GATE]).astype(BF16)
    yg = _dot(h, w_ref[:, COL_GATE:N_IN_PAD])
    for hd in range(N_KV):
        gt = yg[:, hd * LANES:(hd + 1) * LANES].T
        gt_ref[0, hd] = jax.nn.sigmoid(gt[:GATE_PAD])


def _in_projection(x, mod, g, w, cos_t, sin_t, tm, tk):
    b, s, d = x.shape
    n_t = s // tm
    row = lambda n: pl.BlockSpec((1, tm, n), lambda i, j: (i, j, 0))
    vt_spec = pl.BlockSpec((1, tm // tk, N_KV, V_ROWS, tk), lambda i, j: (i, j, 0, 0, 0))
    vt_shape = jax.ShapeDtypeStruct((b, s // tk, N_KV, V_ROWS, tk), BF16)
    return pl.pallas_call(
        _inproj_kernel,
        grid=(b, n_t),
        in_specs=[
            row(d),
            pl.BlockSpec((1, 6, d), lambda i, j: (i, 0, 0)),
            pl.BlockSpec((1, d), lambda i, j: (0, 0)),
            pl.BlockSpec((d, N_IN_PAD), lambda i, j: (0, 0), pipeline_mode=pl.Buffered(1)),
            row(LANES), row(LANES),
        ],
        out_specs=[
            pl.BlockSpec((1, NSA_WIDTH, tm), lambda i, j: (i, 0, j)),
            row(KV_W), row(KV_W), row(KV_W), row(KV_W), vt_spec, vt_spec, row(3 * CONV_WIDTH),
            pl.BlockSpec((1, N_KV, GATE_PAD, tm), lambda i, j: (i, 0, 0, j)),
        ],
        out_shape=[
            jax.ShapeDtypeStruct((b, NSA_WIDTH, s), BF16),
            jax.ShapeDtypeStruct((b, s, KV_W), BF16),
            jax.ShapeDtypeStruct((b, s, KV_W), BF16),
            jax.ShapeDtypeStruct((b, s, KV_W), F32),
            jax.ShapeDtypeStruct((b, s, KV_W), F32),
            vt_shape, vt_shape,
            jax.ShapeDtypeStruct((b, s, 3 * CONV_WIDTH), BF16),
            jax.ShapeDtypeStruct((b, N_KV, GATE_PAD, s), F32),
        ],
        compiler_params=_params("arbitrary", "arbitrary"),
        name="in_projection",
    )(x, mod, g, w, cos_t, sin_t)


def _compress_kernel(tk_ref, tv_ref, pek_ref, pev_ref, w1k_ref, w1v_ref, w2k_ref, w2v_ref,
                     kc_ref, vct_ref):
    n_rows = tk_ref.shape[1] // CMP_STRIDE
    half = CMP_LEN // 2
    acc = [[None, None], [None, None]]
    for l in range(half):
        for kv, (t_ref, pe_ref, w1_ref) in enumerate(((tk_ref, pek_ref, w1k_ref),
                                                      (tv_ref, pev_ref, w1v_ref))):
            tok = t_ref[0, pl.ds(l, n_rows, stride=CMP_STRIDE), :]
            for part in range(2):
                idx = part * half + l
                y = _dot((tok + pe_ref[idx:idx + 1, :]).astype(BF16), w1_ref[idx])
                acc[kv][part] = y if acc[kv][part] is None else acc[kv][part] + y
    outs = []
    for kv, w2_ref in enumerate((w2k_ref, w2v_ref)):
        pre = acc[kv][0] + pltpu.roll(acc[kv][1], n_rows - 1, 0)
        out = _dot(_silu(pre).astype(BF16), w2_ref[...])
        rows = lax.broadcasted_iota(jnp.int32, out.shape, 0)
        outs.append(jnp.where(rows < n_rows - 1, out, 0.0))
    kc_ref[0] = outs[0].astype(BF16)
    vct_ref[0] = outs[1].T.astype(BF16)


def _compress(kcf, vcf, pek, pev, w1k, w1v, w2k, w2v):
    b, s, _ = kcf.shape
    n_rows = s // CMP_STRIDE
    const = lambda a: pl.BlockSpec(a.shape, lambda i: (0,) * a.ndim)
    return pl.pallas_call(
        _compress_kernel,
        grid=(b,),
        in_specs=[pl.BlockSpec((1, s, KV_W), lambda i: (i, 0, 0)),
                  pl.BlockSpec((1, s, KV_W), lambda i: (i, 0, 0)),
                  const(pek), const(pev), const(w1k), const(w1v), const(w2k), const(w2v)],
        out_specs=[pl.BlockSpec((1, n_rows, KV_W), lambda i: (i, 0, 0)),
                   pl.BlockSpec((1, KV_W, n_rows), lambda i: (i, 0, 0))],
        out_shape=[jax.ShapeDtypeStruct((b, n_rows, KV_W), BF16),
                   jax.ShapeDtypeStruct((b, KV_W, n_rows), BF16)],
        compiler_params=_params("arbitrary"),
        name="compress",
    )(kcf, vcf, pek, pev, w1k, w1v, w2k, w2v)


def _attn_kernel(qt_ref, kc_ref, vct_ref, ks_ref, vst_ref, kw_ref, vwt_ref, gt_ref, ov_ref,
                 o_ref, selb_ref, *chain_refs, tq, tk, n_sel):
    t0 = pl.program_id(1) * tq
    t_lane = t0 + lax.broadcasted_iota(jnp.int32, (1, tq), 1)
    parts = tq // QPART
    rows_c = GROUP * QPART
    rows = parts * rows_c
    n_chain = 2 * N_KV * parts
    acc_refs = chain_refs[:n_chain]
    sc_refs = chain_refs[n_chain:]

    def cid(branch, h, part):
        return (branch * N_KV + h) * parts + part

    def part_cols(m, a):
        return m[:, a * QPART:(a + 1) * QPART]

    def over_heads(m):
        return jnp.concatenate([m] * GROUP, axis=1)

    def to_lanes(m):
        return jnp.concatenate([over_heads(part_cols(m, a)) for a in range(parts)], axis=1)

    def compressed_branch(h, qt):
        n_cmp = kc_ref.shape[1]
        s_c = _dot(kc_ref[0], qt)
        cmp_end = lax.broadcasted_iota(jnp.int32, (n_cmp, 1), 0) * CMP_STRIDE + (CMP_LEN - 1)
        s_c = jnp.where(cmp_end <= to_lanes(t_lane), s_c, -jnp.inf)
        m_c = jnp.max(s_c, axis=0, keepdims=True)
        m_c = jnp.where(m_c == -jnp.inf, 0.0, m_c)
        p_c = jnp.exp2(s_c - m_c)
        p_c = p_c * (1.0 / jnp.maximum(jnp.sum(p_c, axis=0, keepdims=True), 1e-30))
        o_c = _dot(vct_ref[0, h * HEAD_DIM:(h + 1) * HEAD_DIM, :], p_c.astype(BF16))
        p_sum = []
        for a in range(parts):
            blocks = [p_c[:, a * rows_c + g * QPART:a * rows_c + (g + 1) * QPART]
                      for g in range(GROUP)]
            p_sum.append(sum(blocks[1:], blocks[0]))
        p_sum = jnp.concatenate(p_sum, axis=1)
        p_hi = p_sum.astype(BF16)
        p_lo = (p_sum - p_hi.astype(F32)).astype(BF16)
        imp = _dot(ov_ref[...], p_hi) + _dot(ov_ref[...], p_lo)

        n_blk = imp.shape[0]
        j_blk = lax.broadcasted_iota(jnp.int32, (n_blk, tq), 0).astype(F32)
        blk_q = (t_lane // SEL_BLOCK).astype(F32)
        causal = j_blk <= blk_q
        forced = causal & ((j_blk == 0.0) | (j_blk == blk_q) | (j_blk == blk_q - 1.0))
        val = jnp.where(forced, jnp.inf, jnp.where(causal, imp, -jnp.inf))
        bias = jnp.full((n_blk, tq), NEG, F32)
        for _ in range(n_sel):
            top = jnp.max(val, axis=0, keepdims=True)
            cand = (val == top) & (val > -jnp.inf)
            first = jnp.min(jnp.where(cand, j_blk, float(n_blk)), axis=0, keepdims=True)
            pick = j_blk == first
            bias = jnp.where(pick, 0.0, bias)
            val = jnp.where(pick, -jnp.inf, val)
        for a in range(parts):
            selb_ref[h, a] = part_cols(bias, a)
        return o_c

    qts, o_cs = [], []
    zeros = jnp.zeros((HEAD_DIM, rows), BF16)
    for h in range(N_KV):
        qh = jnp.concatenate(
            [qt_ref[0, (h * GROUP + g) * HEAD_DIM:(h * GROUP + g + 1) * HEAD_DIM,
                    a * QPART:(a + 1) * QPART]
             for a in range(parts) for g in range(GROUP)], axis=1)
        qt = jnp.concatenate([qh, zeros] if h == 0 else [zeros, qh], axis=0)
        qts.append(qt)
        o_cs.append(compressed_branch(h, qt))

    def scores(c, k_rows, h, a, bias):
        n = k_rows.shape[0]
        s_ref = sc_refs[c]
        s = _dot(k_rows, qts[h][:, a * rows_c:(a + 1) * rows_c]) + over_heads(bias)
        s_ref[0:n, :] = s
        m_tile = jnp.max(s, axis=0, keepdims=True)
        return m_tile, jnp.exp2(s_ref[0:n, :] - m_tile).astype(BF16)

    def flash_step(m_tile, p, vt_cols, m, acc_ref):
        if m is None:
            acc_ref[...] = _dot(vt_cols, p)
            return m_tile
        m_new = jnp.maximum(m, m_tile)
        alpha = jnp.exp2(m - m_new)
        beta = jnp.exp2(m_tile - m_new)
        acc_ref[...] = alpha * acc_ref[...] + beta * _dot(vt_cols, p)
        return m_new

    hi_tile = (t0 + tq - 1) // tk
    per_blk = tk // SEL_BLOCK

    def block_bias(h, kt, a, r0, r1):
        return jnp.concatenate(
            [jnp.broadcast_to(selb_ref[h, a, pl.ds(kt * per_blk + jj, 1), :], (SEL_BLOCK, QPART))
             for jj in range(r0 // SEL_BLOCK, r1 // SEL_BLOCK)], axis=0)

    def key_dist(kt, a, r0, r1):
        u = kt * tk + r0 + lax.broadcasted_iota(jnp.int32, (r1 - r0, 1), 0)
        return part_cols(t_lane, a) - u

    def k_tile(k_ref, kt):
        return k_ref[0, pl.ds(pl.multiple_of(kt * tk, tk), tk), :]

    def run_chains(jobs, ms):
        out = list(ms)
        waiting = None
        for c, h, a, k_rows, bias, vt_cols in jobs:
            res = scores(c, k_rows, h, a, bias)
            if waiting is not None:
                c0, res0, vt0 = waiting
                out[c0] = flash_step(*res0, vt0, ms[c0], acc_refs[c0])
            waiting = (c, res, vt_cols)
        c0, res0, vt0 = waiting
        out[c0] = flash_step(*res0, vt0, ms[c0], acc_refs[c0])
        return tuple(out)

    def tile_jobs(kt, sel_rows, win_rows, diagonal):
        k_s = k_tile(ks_ref, kt)
        sel_jobs, win_jobs = [], []
        for h in range(N_KV):
            for a in range(parts):
                r0, r1 = sel_rows(a)
                bias = block_bias(h, kt, a, r0, r1)
                if diagonal:
                    bias = jnp.where(key_dist(kt, a, r0, r1) >= 0, bias, NEG)
                sel_jobs.append((cid(0, h, a), h, a, k_s[r0:r1], bias,
                                 vst_ref[0, kt, h, :, r0:r1]))
        if win_rows is not None:
            k_w = k_tile(kw_ref, kt)
            for h in range(N_KV):
                for a in range(parts):
                    r0, r1 = win_rows(a)
                    dist = key_dist(kt, a, r0, r1)
                    bias = jnp.where((dist >= 0) & (dist < WINDOW), 0.0, NEG)
                    win_jobs.append((cid(1, h, a), h, a, k_w[r0:r1], bias,
                                     vwt_ref[0, kt, h, :, r0:r1]))
        return win_jobs + sel_jobs if diagonal else sel_jobs + win_jobs

    diag = lambda a: (0, (a + 1) * QPART)
    whole = lambda a: (0, tk)
    ms = run_chains(tile_jobs(hi_tile, diag, diag, True), (None,) * n_chain)

    def prev_body(_, ms):
        return run_chains(tile_jobs(hi_tile - 1, whole, lambda a: (a * QPART, tk), False), ms)

    ms = lax.fori_loop(0, jnp.minimum(hi_tile, 1), prev_body, ms)

    def far_body(i, ms):
        return run_chains(tile_jobs(hi_tile - i, whole, None, False), ms)

    lax.fori_loop(2, hi_tile + 1, far_body, ms[:N_KV * parts])

    heads = []
    for h in range(N_KV):
        a_s = jnp.concatenate([acc_refs[cid(0, h, a)][...] for a in range(parts)], axis=1)
        a_w = jnp.concatenate([acc_refs[cid(1, h, a)][...] for a in range(parts)], axis=1)
        o_s = a_s[:HEAD_DIM] * (1.0 / jnp.maximum(a_s[HEAD_DIM:HEAD_DIM + 1], 1e-30))
        o_w = a_w[:HEAD_DIM] * (1.0 / jnp.maximum(a_w[HEAD_DIM:HEAD_DIM + 1], 1e-30))
        gate = gt_ref[0, h]
        for g in range(GROUP):
            blocks = []
            for a in range(parts):
                sl = slice(a * rows_c + g * QPART, a * rows_c + (g + 1) * QPART)
                gate_a = part_cols(gate, a)
                blocks.append(gate_a[g:g + 1] * o_cs[h][:, sl]
                              + gate_a[GROUP + g:GROUP + g + 1] * o_s[:, sl]
                              + gate_a[2 * GROUP + g:2 * GROUP + g + 1] * o_w[:, sl])
            heads.append(jnp.concatenate(blocks, axis=1))
    for pair in range(N_HEADS // 2):
        both = jnp.concatenate(heads[2 * pair:2 * pair + 2], axis=0)
        o_ref[0, :, pair * LANES:(pair + 1) * LANES] = both.T.astype(BF16)


def _attention(qt, kc, vct, ks, vst, kw, vwt, gt, ov, tq, tk):
    b, _, s = qt.shape
    n_cmp = kc.shape[1]
    n_kt = s // tk
    n_blk = ov.shape[0]
    n_sel = min(N_SEL, s // SEL_BLOCK)
    assert tq == tk == WINDOW, "the kernel's tile walk assumes query tile = key tile = window"
    n_chain = 2 * N_KV * (tq // QPART)
    kern = functools.partial(_attn_kernel, tq=tq, tk=tk, n_sel=n_sel)
    k_spec = pl.BlockSpec((1, s, KV_W), lambda i, j: (i, 0, 0))
    vt_spec = pl.BlockSpec((1, n_kt, N_KV, V_ROWS, tk), lambda i, j: (i, 0, 0, 0, 0))
    acc = pltpu.VMEM((V_ROWS, GROUP * QPART), F32)
    score = pltpu.VMEM((tk, GROUP * QPART), F32)
    return pl.pallas_call(
        kern,
        grid=(b, s // tq),
        in_specs=[
            pl.BlockSpec((1, NSA_WIDTH, tq), lambda i, j: (i, 0, j)),
            pl.BlockSpec((1, n_cmp, KV_W), lambda i, j: (i, 0, 0)),
            pl.BlockSpec((1, KV_W, n_cmp), lambda i, j: (i, 0, 0)),
            k_spec, vt_spec, k_spec, vt_spec,
            pl.BlockSpec((1, N_KV, GATE_PAD, tq), lambda i, j: (i, 0, 0, j)),
            pl.BlockSpec((n_blk, n_cmp), lambda i, j: (0, 0)),
        ],
        out_specs=pl.BlockSpec((1, tq, NSA_WIDTH), lambda i, j: (i, j, 0)),
        out_shape=jax.ShapeDtypeStruct((b, s, NSA_WIDTH), BF16),
        scratch_shapes=[pltpu.VMEM((N_KV, tq // QPART, n_blk, QPART), F32)]
        + [acc] * n_chain + [score] * n_chain,
        compiler_params=_params("arbitrary", "arbitrary"),
        name="nsa_attention",
    )(qt, kc, vct, ks, vst, kw, vwt, gt, ov)


def _shifted_taps(cur, prev, taps):
    t = cur.shape[0]
    row = lax.broadcasted_iota(jnp.int32, (8, 1), 0)

    def shifted(k):
        body = pltpu.roll(cur, k, 0)
        head = jnp.where(row < k, pltpu.roll(prev, k, 0), body[:8])
        return jnp.concatenate([head, body[8:]], axis=0)

    return taps[0:1] * shifted(2) + taps[1:2] * shifted(1) + taps[2:3] * cur


def _group_norm(v, g):
    r = lax.rsqrt(jnp.mean(v * v, axis=-1, keepdims=True) + EPS)
    return (v * r) * g


def _outproj_kernel(attn_ref, cb_ref, cc_ref, ch_ref, ccp_ref, chp_ref, x_ref, mod_ref,
                    ga_ref, gc_ref, cw_ref, w_ref, o_ref):
    f32 = lambda ref: ref[0].astype(F32)
    z = f32(cc_ref) * f32(ch_ref)
    z_halo = f32(ccp_ref) * f32(chp_ref)
    z_prev = jnp.where(pl.program_id(1) > 0, z_halo[8:], 0.0)
    conv = f32(cb_ref) * _shifted_taps(z, z_prev, cw_ref[...])
    mix_a = _group_norm(f32(attn_ref), ga_ref[...]).astype(BF16)
    mix_c = _group_norm(conv, gc_ref[...]).astype(BF16)
    y = _dot(mix_a, w_ref[:NSA_WIDTH, :]) + _dot(mix_c, w_ref[NSA_WIDTH:, :])
    o_ref[0] = x_ref[0] + mod_ref[0, 2:3, :] * y


def _out_projection(attn, conv3, x, mod, ga, gc, cw, w, tm):
    b, s, d = x.shape
    halo = tm // HALO_ROWS
    col = lambda c: pl.BlockSpec((1, tm, CONV_WIDTH), lambda i, j: (i, j, c))
    prev = lambda c: pl.BlockSpec((1, HALO_ROWS, CONV_WIDTH),
                                  lambda i, j: (i, jnp.maximum(j * halo - 1, 0), c))
    const = lambda shape: pl.BlockSpec(shape, lambda i, j: (0,) * len(shape))
    return pl.pallas_call(
        _outproj_kernel,
        grid=(b, s // tm),
        in_specs=[
            pl.BlockSpec((1, tm, NSA_WIDTH), lambda i, j: (i, j, 0)),
            col(0), col(1), col(2), prev(1), prev(2),
            pl.BlockSpec((1, tm, d), lambda i, j: (i, j, 0)),
            pl.BlockSpec((1, 6, d), lambda i, j: (i, 0, 0)),
            const((1, NSA_WIDTH)), const((1, CONV_WIDTH)), const((CONV_K, CONV_WIDTH)),
            const((NSA_WIDTH + CONV_WIDTH, d)),
        ],
        out_specs=pl.BlockSpec((1, tm, d), lambda i, j: (i, j, 0)),
        out_shape=jax.ShapeDtypeStruct((b, s, d), F32),
        compiler_params=_params("arbitrary", "arbitrary"),
        name="out_projection",
    )(attn, conv3, conv3, conv3, conv3, conv3, x, mod, ga, gc, cw, w)


def _ffn_kernel(x_ref, mod_ref, g_ref, wu_ref, cw_ref, wd_ref, fg_ref, o_ref,
                act_ref, carry_ref, *, d_ff, tf, final_norm):
    x = x_ref[0]
    h = _rms_mod(x, g_ref[...], mod_ref[0, 4:5, :], mod_ref[0, 3:4, :]).astype(BF16)

    @pl.when(pl.program_id(1) == 0)
    def _():
        carry_ref[...] = jnp.zeros_like(carry_ref)

    def conv_cols(cols, tap_scale):
        u = _dot(h, wu_ref[:, cols])
        prev = carry_ref[:, cols]
        carry_ref[:, cols] = u[u.shape[0] - 8:, :]
        taps = cw_ref[:, cols]
        return _shifted_taps(u, prev, taps if tap_scale == 1.0 else taps * tap_scale)

    for c in range(d_ff // tf):
        half_gate = conv_cols(slice(c * tf, (c + 1) * tf), 0.5)
        value = conv_cols(slice(d_ff + c * tf, d_ff + (c + 1) * tf), 1.0)
        act_ref[:, c * tf:(c + 1) * tf] = (_silu_of_half(half_gate) * value).astype(BF16)
    y = _dot(act_ref[...], wd_ref[...])
    out = x + mod_ref[0, 5:6, :] * y
    if final_norm:
        out = _group_norm(out, fg_ref[...])
    o_ref[0] = out


def _ffn(x, mod, g, wu, cw, wd, fg, tm, final_norm):
    b, s, d = x.shape
    d_ff = wd.shape[0]
    tf = 256
    kern = functools.partial(_ffn_kernel, d_ff=d_ff, tf=tf, final_norm=final_norm)
    const = lambda shape: pl.BlockSpec(shape, lambda i, j: (0,) * len(shape),
                                       pipeline_mode=pl.Buffered(1))
    return pl.pallas_call(
        kern,
        grid=(b, s // tm),
        in_specs=[
            pl.BlockSpec((1, tm, d), lambda i, j: (i, j, 0)),
            pl.BlockSpec((1, 6, d), lambda i, j: (i, 0, 0)),
            const((1, d)), const((d, 2 * d_ff)), const((CONV_K, 2 * d_ff)), const((d_ff, d)),
            const((1, d)),
        ],
        out_specs=pl.BlockSpec((1, tm, d), lambda i, j: (i, j, 0)),
        out_shape=jax.ShapeDtypeStruct((b, s, d), F32),
        scratch_shapes=[pltpu.VMEM((tm, d_ff), BF16), pltpu.VMEM((8, 2 * d_ff), F32)],
        compiler_params=_params("arbitrary", "arbitrary"),
        name="conv_ffn",
    )(x, mod, g, wu, cw, wd, fg)


def _reorder_w_in(w):
    d = w.shape[0]
    o_gate = NSA_WIDTH + 6 * KV_W
    o_conv = o_gate + N_HEADS * N_BRANCH
    gates = w[:, o_gate:o_conv].reshape(d, N_KV, GROUP, N_BRANCH)
    gates = gates.transpose(0, 1, 3, 2).reshape(d, N_KV, N_BRANCH * GROUP)
    gates = jnp.pad(gates, ((0, 0), (0, 0), (0, LANES - N_BRANCH * GROUP)))
    return jnp.concatenate(
        [w[:, :o_gate], w[:, o_conv:], gates.reshape(d, N_KV * LANES)], axis=1).astype(BF16)


def _two_head_blockdiag(w):
    z = jnp.zeros_like(w)
    return jnp.concatenate([jnp.concatenate([w, z], axis=-1),
                            jnp.concatenate([z, w], axis=-1)], axis=-2)


def _compress_weights(pe, w1, w2):
    pe2 = jnp.concatenate([pe, pe], axis=1)
    w1_2 = _two_head_blockdiag(w1.reshape(CMP_LEN, HEAD_DIM, CMP_HIDDEN)).astype(BF16)
    w2_2 = _two_head_blockdiag(w2).astype(BF16)
    return pe2, w1_2, w2_2


def _overlap_matrix(s):
    n_cmp = s // CMP_STRIDE
    n_blk = max(s // SEL_BLOCK, 8)
    cmp_start = jnp.arange(n_cmp) * CMP_STRIDE
    blk_start = jnp.arange(n_blk) * SEL_BLOCK
    ov = ((cmp_start[None, :] < blk_start[:, None] + SEL_BLOCK)
          & (cmp_start[None, :] + CMP_LEN > blk_start[:, None])
          & (jnp.arange(n_cmp)[None, :] < n_cmp - 1))
    return ov.astype(BF16)


def kernel(x, c, positions, w_mod, b_mod, norm1_g, w_in, cmp_pe_k, cmp_w1_k, cmp_w2_k,
           cmp_pe_v, cmp_w1_v, cmp_w2_v, conv_w, grp_g_attn, grp_g_conv, w_out, norm2_g,
           ffn_up, ffn_conv, ffn_down, final_g):
    b, s, d = x.shape
    depth = w_in.shape[0]
    tm = min(512, s)
    tq = min(512, s)
    tm_proj = min(1024, s)

    mod_all = _modulation(c, w_mod, b_mod).reshape(depth, b, 6, d)
    cos_t, sin_t = _rope_tables(positions)
    ov = _overlap_matrix(s)

    for l in range(depth):
        mod = mod_all[l]
        qt, ks, kw, kcf, vcf, vst, vwt, conv3, gt = _in_projection(
            x, mod, norm1_g[l].reshape(1, d), _reorder_w_in(w_in[l]), cos_t, sin_t, tm_proj, tm)
        pek, w1k, w2k = _compress_weights(cmp_pe_k[l], cmp_w1_k[l], cmp_w2_k[l])
        pev, w1v, w2v = _compress_weights(cmp_pe_v[l], cmp_w1_v[l], cmp_w2_v[l])
        kc, vct = _compress(kcf, vcf, pek, pev, w1k, w1v, w2k, w2v)
        attn = _attention(qt, kc, vct, ks, vst, kw, vwt, gt, ov, tq, tm)
        x = _out_projection(attn, conv3, x, mod, grp_g_attn[l].reshape(1, -1),
                            grp_g_conv[l].reshape(1, -1), conv_w[l], w_out[l].astype(BF16),
                            tm_proj)
        x = _ffn(x, mod, norm2_g[l].reshape(1, d), ffn_up[l].astype(BF16), ffn_conv[l],
                 ffn_down[l].astype(BF16), final_g.reshape(1, d), tm, l == depth - 1)
    return x
```
